```python
import math
import jax, jax.numpy as jnp
from jax import lax
import numpy as np

D_MODEL = 4096
BATCH = 8
SEQ = 2048
DEPTH = 2
DEC_BATCH = 32
DEC_SEQ = 32
PAST_LEN = 4096

CHUNK = 64
Q_BLOCK = 128
N_MEM = 256

D_MIX = D_MODEL
D_GLA = 3 * D_MIX // 8
D_POOL = D_MIX // 4
D_DIFF = D_MIX - D_GLA - D_POOL

GLA_HEADS = 4
GLA_DV = D_GLA // GLA_HEADS
GLA_DK = GLA_DV // 2
GLA_KW = GLA_HEADS * GLA_DK
GLA_LR = 16
GLA_TAU = 16.0

POOL_WINDOWS = (2, 4, 8, 16)
POOL_GW = D_POOL // len(POOL_WINDOWS)
POOL_HIST = max(POOL_WINDOWS) - 1

DIFF_DH = 128
DIFF_HEADS = D_DIFF // (2 * DIFF_DH)

MEM_HEADS = 4
MEM_DH = D_MODEL // MEM_HEADS

D_FF = ((8 * D_MODEL + 3 * 256 - 1) // (3 * 256)) * 256

ROPE_THETA = 10000.0
ALPHA = (2 * DEPTH) ** 0.25
BETA = (8 * DEPTH) ** -0.25
NEG_INF = -1e30

IN_SPLITS = (GLA_KW, GLA_KW, D_GLA, D_GLA, GLA_LR, D_DIFF, D_DIFF, D_DIFF, D_POOL)
N_IN = sum(IN_SPLITS)

kernel_name = "hymba_gla_pool_diffattn_stream_step"


def layer_norm(x, g, b, eps=1e-5):
    xf = x.astype(jnp.float32)
    mu = jnp.mean(xf, axis=-1, keepdims=True)
    var = jnp.mean(jnp.square(xf - mu), axis=-1, keepdims=True)
    return ((xf - mu) * lax.rsqrt(var + eps) * g + b).astype(x.dtype)


def rms_norm_heads(x, g, eps=1e-6):
    xf = x.astype(jnp.float32)
    xf = xf * lax.rsqrt(jnp.mean(xf * xf, axis=-1, keepdims=True) + eps)
    return (xf.reshape(*x.shape[:-2], -1) * g).astype(x.dtype)


def rotary(x, pos):
    dh = x.shape[-1]
    inv = ROPE_THETA ** (-jnp.arange(0, dh, 2, dtype=jnp.float32) / dh)
    ang = pos.astype(jnp.float32)[:, None] * inv[None, :]
    shape = (x.shape[1],) + (1,) * (x.ndim - 3) + (dh // 2,)
    cos = jnp.cos(ang).reshape(shape)
    sin = jnp.sin(ang).reshape(shape)
    x1, x2 = jnp.split(x.astype(jnp.float32), 2, axis=-1)
    return jnp.concatenate([x1 * cos - x2 * sin, x2 * cos + x1 * sin], axis=-1).astype(x.dtype)


def split_cols(z):
    parts, start = [], 0
    for w in IN_SPLITS:
        parts.append(z[..., start:start + w])
        start += w
    return parts


def gla_block(S, blk):
    q, k, v, lg = blk
    L = q.shape[1]
    b = jnp.cumsum(lg.astype(jnp.float32), axis=1)
    qf = q.astype(jnp.float32) * jnp.exp(b) * (GLA_DK ** -0.5)
    kf = k.astype(jnp.float32)
    vf = v.astype(jnp.float32)
    causal = jnp.tril(jnp.ones((L, L), dtype=bool))
    a = jnp.einsum("blhk,bmhk->bhlm", qf, kf * jnp.exp(-b))
    a = jnp.where(causal, a, 0.0)
    o = jnp.einsum("blhk,bhkv->blhv", qf, S) + jnp.einsum("bhlm,bmhv->blhv", a, vf)
    b_last = b[:, -1]
    k_dec = kf * jnp.exp(b_last[:, None] - b)
    S_new = S * jnp.exp(b_last)[..., None] + jnp.einsum("blhk,blhv->bhkv", k_dec, vf)
    return S_new, o.astype(v.dtype)


def gla_prompt(q, k, v, lg):
    B, T, H, dk = q.shape
    dv = v.shape[-1]
    nc = T // CHUNK
    to_blocks = lambda a: jnp.moveaxis(a.reshape(B, nc, CHUNK, *a.shape[2:]), 1, 0)
    S0 = jnp.zeros((B, H, dk, dv), jnp.float32)
    S_fin, o = lax.scan(gla_block, S0, (to_blocks(q), to_blocks(k), to_blocks(v), to_blocks(lg)))
    return jnp.moveaxis(o, 0, 1).reshape(B, T, H, dv), S_fin


def pool_mix(u, hist, pos, w_pool, scale):
    B, T, _ = u.shape
    full = jnp.concatenate([hist.astype(u.dtype), u], axis=1)
    ff = full.astype(jnp.float32)
    cs = jnp.concatenate([jnp.zeros((B, 1, D_POOL), jnp.float32), jnp.cumsum(ff, axis=1)], axis=1)
    top = cs[:, POOL_HIST + 1:]
    cur = ff[:, POOL_HIST:]
    outs = []
    for gi, w in enumerate(POOL_WINDOWS):
        sl = slice(gi * POOL_GW, (gi + 1) * POOL_GW)
        low = cs[:, POOL_HIST + 1 - w:POOL_HIST + 1 - w + T, sl]
        cnt = jnp.minimum(pos + 1, w).astype(jnp.float32)[None, :, None]
        outs.append((top[..., sl] - low) / cnt - cur[..., sl])
    pooled = jnp.stack(outs, axis=2)
    y = jnp.einsum("btgc,gcd->btgd", pooled, w_pool).reshape(B, T, D_POOL) * scale
    return y.astype(u.dtype), full[:, -POOL_HIST:]


def diff_attn_core(q, k, v, mask, lam):
    s = jnp.einsum("bqhmd,bkhmd->bhmqk", q, k).astype(jnp.float32) * (DIFF_DH ** -0.5)
    s = jnp.where(mask, s, NEG_INF)
    p = jax.nn.softmax(s, axis=-1)
    p = p[:, :, 0] - lam * p[:, :, 1]
    return jnp.einsum("bhqk,bkhv->bqhv", p.astype(v.dtype), v)


def diff_attn_prompt(q, k, v, lam):
    B, T = q.shape[:2]
    nblk = T // Q_BLOCK
    qb = jnp.moveaxis(q.reshape(B, nblk, Q_BLOCK, *q.shape[2:]), 1, 0)
    k_chunk = jnp.arange(T) // CHUNK

    def one_block(args):
        qi, i = args
        q_chunk = (i * Q_BLOCK + jnp.arange(Q_BLOCK)) // CHUNK
        mask = k_chunk[None, :] <= q_chunk[:, None]
        return diff_attn_core(qi, k, v, mask, lam)

    o = lax.map(one_block, (qb, jnp.arange(nblk)))
    return jnp.moveaxis(o, 0, 1).reshape(B, T, DIFF_HEADS, 2 * DIFF_DH)


def hybrid_mixer(h, pos, p, layer_idx, gla_S, pool_hist, k_past, v_past, prompt):
    B, T, _ = h.shape
    q_g, k_g, v_g, r_g, a_lr, q_d, k_d, v_d, u_p = split_cols(h @ p["w_in"])

    q_g = q_g.reshape(B, T, GLA_HEADS, GLA_DK)
    k_g = k_g.reshape(B, T, GLA_HEADS, GLA_DK)
    v_g = v_g.reshape(B, T, GLA_HEADS, GLA_DV)
    lg = jax.nn.log_sigmoid((a_lr @ p["gla_w_a"] + p["gla_b_a"]).astype(jnp.float32)) / GLA_TAU
    lg = lg.reshape(B, T, GLA_HEADS, GLA_DK)
    if prompt:
        o_g, S_new = gla_prompt(q_g, k_g, v_g, lg)
    else:
        S_new, o_g = gla_block(gla_S, (q_g, k_g, v_g, lg))
    y_gla = rms_norm_heads(o_g, p["gla_norm"]) * jax.nn.silu(r_g)

    y_pool, hist_new = pool_mix(u_p, pool_hist, pos, p["pool_w"], p["pool_scale"])

    q_d = rotary(q_d.reshape(B, T, DIFF_HEADS, 2, DIFF_DH), pos)
    k_d = rotary(k_d.reshape(B, T, DIFF_HEADS, 2, DIFF_DH), pos)
    v_d = v_d.reshape(B, T, DIFF_HEADS, 2 * DIFF_DH)
    lam_init = 0.8 - 0.6 * math.exp(-0.3 * layer_idx)
    lp = p["diff_lambda"].astype(jnp.float32)
    lam = jnp.exp(jnp.sum(lp[0] * lp[1])) - jnp.exp(jnp.sum(lp[2] * lp[3])) + lam_init
    if prompt:
        o_d = diff_attn_prompt(q_d, k_d, v_d, lam)
    else:
        k_all = jnp.concatenate([k_past.astype(k_d.dtype), k_d], axis=1)
        v_all = jnp.concatenate([v_past.astype(v_d.dtype), v_d], axis=1)
        mask = jnp.ones((T, k_all.shape[1]), dtype=bool)
        o_d = diff_attn_core(q_d, k_all, v_all, mask, lam)
    y_diff = rms_norm_heads(o_d, p["diff_norm"]) * (1.0 - lam_init)

    y = jnp.concatenate([y_gla, y_pool, y_diff], axis=-1) @ p["w_out"]
    return y, S_new, hist_new, k_d, v_d


def memory_attention(h, mem_k, mem_v, w_q, w_o):
    B, T, _ = h.shape
    q = (h @ w_q).reshape(B, T, MEM_HEADS, MEM_DH)
    s = jnp.einsum("bthd,bmhd->bhtm", q, mem_k).astype(jnp.float32) * (MEM_DH ** -0.5)
    pr = jax.nn.softmax(s, axis=-1).astype(mem_v.dtype)
    o = jnp.einsum("bhtm,bmhd->bthd", pr, mem_v).reshape(B, T, MEM_HEADS * MEM_DH)
    return o @ w_o


def swiglu(h, w_gate, w_up, w_down):
    return (jax.nn.silu(h @ w_gate) * (h @ w_up)) @ w_down


def trunk_layer(x, pos, p, layer_idx, gla_S, pool_hist, k_past, v_past, mem_k, mem_v, prompt):
    y_mix, S_new, hist_new, k_new, v_new = hybrid_mixer(x, pos, p, layer_idx, gla_S, pool_hist, k_past, v_past, prompt)
    x = layer_norm(ALPHA * x + y_mix, p["ln_g"][0], p["ln_b"][0])
    x = layer_norm(ALPHA * x + memory_attention(x, mem_k, mem_v, p["w_mq"], p["w_mo"]), p["ln_g"][1], p["ln_b"][1])
    x = layer_norm(ALPHA * x + swiglu(x, p["w_gate"], p["w_up"], p["w_down"]), p["ln_g"][2], p["ln_b"][2])
    return x, S_new, hist_new, k_new, v_new


def setup_inputs(seed: int = 0) -> dict:
    key = jax.random.key(seed)
    ks = jax.random.split(key, 32)

    def nrm(k, shape, scale=1.0):
        return jax.random.normal(k, shape, jnp.float32) * scale

    return {
        "x_prompt": nrm(ks[0], (BATCH, SEQ, D_MODEL)),
        "x_sample": nrm(ks[1], (DEC_BATCH, DEC_SEQ, D_MODEL)),
        "cache_diff_k": nrm(ks[2], (DEPTH, DEC_BATCH, PAST_LEN, DIFF_HEADS, 2, DIFF_DH)),
        "cache_diff_v": nrm(ks[3], (DEPTH, DEC_BATCH, PAST_LEN, DIFF_HEADS, 2 * DIFF_DH)),
        "state_gla": nrm(ks[4], (DEPTH, DEC_BATCH, GLA_HEADS, GLA_DK, GLA_DV), 0.1),
        "state_pool": nrm(ks[5], (DEPTH, DEC_BATCH, POOL_HIST, D_POOL)),
        "cache_mem_k": nrm(ks[6], (DEPTH, DEC_BATCH, N_MEM, MEM_HEADS, MEM_DH)),
        "cache_mem_v": nrm(ks[7], (DEPTH, DEC_BATCH, N_MEM, MEM_HEADS, MEM_DH)),
        "mem_prompt": nrm(ks[8], (BATCH, N_MEM, D_MODEL)),
        "w_in": nrm(ks[9], (DEPTH, D_MODEL, N_IN), D_MODEL ** -0.5),
        "gla_w_a": nrm(ks[10], (DEPTH, GLA_LR, GLA_KW), GLA_LR ** -0.5),
        "gla_b_a": nrm(ks[11], (DEPTH, GLA_KW), 0.1),
        "gla_norm": 1.0 + nrm(ks[12], (DEPTH, D_GLA), 0.01),
        "pool_w": nrm(ks[13], (DEPTH, len(POOL_WINDOWS), POOL_GW, POOL_GW), POOL_GW ** -0.5),
        "pool_scale": 1.0 + nrm(ks[14], (DEPTH, D_POOL), 0.01),
        "diff_lambda": nrm(ks[15], (DEPTH, 4, DIFF_DH), 0.1),
        "diff_norm": 1.0 + nrm(ks[16], (DEPTH, D_DIFF), 0.01),
        "w_out": nrm(ks[17], (DEPTH, D_MIX, D_MODEL), BETA * D_MIX ** -0.5),
        "w_mq": nrm(ks[18], (DEPTH, D_MODEL, MEM_HEADS * MEM_DH), D_MODEL ** -0.5),
        "w_mk": nrm(ks[19], (DEPTH, D_MODEL, MEM_HEADS * MEM_DH), D_MODEL ** -0.5),
        "w_mv": nrm(ks[20], (DEPTH, D_MODEL, MEM_HEADS * MEM_DH), D_MODEL ** -0.5),
        "w_mo": nrm(ks[21], (DEPTH, MEM_HEADS * MEM_DH, D_MODEL), BETA * (MEM_HEADS * MEM_DH) ** -0.5),
        "w_gate": nrm(ks[22], (DEPTH, D_MODEL, D_FF), D_MODEL ** -0.5),
        "w_up": nrm(ks[23], (DEPTH, D_MODEL, D_FF), D_MODEL ** -0.5),
        "w_down": nrm(ks[24], (DEPTH, D_FF, D_MODEL), BETA * D_FF ** -0.5),
        "ln_g": 1.0 + nrm(ks[25], (DEPTH, 3, D_MODEL), 0.01),
        "ln_b": nrm(ks[26], (DEPTH, 3, D_MODEL), 0.01),
    }


def reference(x_prompt, x_sample, cache_diff_k, cache_diff_v, state_gla, state_pool, cache_mem_k, cache_mem_v,
              mem_prompt, w_in, gla_w_a, gla_b_a, gla_norm, pool_w, pool_scale, diff_lambda, diff_norm, w_out,
              w_mq, w_mk, w_mv, w_mo, w_gate, w_up, w_down, ln_g, ln_b):
    Bp, Tp, _ = x_prompt.shape
    Ts = x_sample.shape[1]
    pos_p = jnp.arange(Tp)
    pos_s = PAST_LEN + jnp.arange(Ts)
    hp, hs = x_prompt, x_sample
    kp_l, vp_l, Sp_l, hist_p_l, mk_l, mv_l = [], [], [], [], [], []
    ks_l, vs_l, Ss_l, hist_s_l = [], [], [], []
    for l in range(DEPTH):
        p = {
            "w_in": w_in[l], "gla_w_a": gla_w_a[l], "gla_b_a": gla_b_a[l], "gla_norm": gla_norm[l],
            "pool_w": pool_w[l], "pool_scale": pool_scale[l], "diff_lambda": diff_lambda[l],
            "diff_norm": diff_norm[l], "w_out": w_out[l], "w_mq": w_mq[l], "w_mo": w_mo[l],
            "w_gate": w_gate[l], "w_up": w_up[l], "w_down": w_down[l], "ln_g": ln_g[l], "ln_b": ln_b[l],
        }
        mk_p = (mem_prompt @ w_mk[l]).reshape(Bp, N_MEM, MEM_HEADS, MEM_DH)
        mv_p = (mem_prompt @ w_mv[l]).reshape(Bp, N_MEM, MEM_HEADS, MEM_DH)
        hist0 = jnp.zeros((Bp, POOL_HIST, D_POOL), x_prompt.dtype)
        hp, S_p, hist_p, k_p, v_p = trunk_layer(hp, pos_p, p, l, None, hist0, None, None, mk_p, mv_p, True)
        hs, S_s, hist_s, k_s, v_s = trunk_layer(hs, pos_s, p, l, state_gla[l], state_pool[l], cache_diff_k[l],
                                                cache_diff_v[l], cache_mem_k[l], cache_mem_v[l], False)
        kp_l.append(k_p); vp_l.append(v_p); Sp_l.append(S_p); hist_p_l.append(hist_p)
        mk_l.append(mk_p); mv_l.append(mv_p)
        ks_l.append(k_s); vs_l.append(v_s); Ss_l.append(S_s); hist_s_l.append(hist_s)
    return (hp, hs,
            jnp.stack(kp_l), jnp.stack(vp_l), jnp.stack(Sp_l), jnp.stack(hist_p_l),
            jnp.stack(mk_l), jnp.stack(mv_l),
            jnp.stack(ks_l), jnp.stack(vs_l), jnp.stack(Ss_l), jnp.stack(hist_s_l))
```

```python
import functools
import math

import jax
import jax.numpy as jnp
from jax import lax
from jax.experimental import pallas as pl
from jax.experimental.pallas import tpu as pltpu

F32 = jnp.float32
BF16 = jnp.bfloat16

LANE = 128
V7X_VMEM_BYTES = 64 * 1024 * 1024
VMEM_LIMIT = V7X_VMEM_BYTES - 8 * 1024 * 1024

CHUNK = 64
GLA_HEADS = 4
GLA_TAU = 16.0
POOL_WINDOWS = (2, 4, 8, 16)
POOL_HIST = max(POOL_WINDOWS) - 1
DIFF_DH = 128
ROPE_THETA = 10000.0
NEG_INF = -1e30
LN_EPS = 1e-5
RMS_EPS = 1e-6


def _params(sem):
    return pltpu.CompilerParams(dimension_semantics=sem, vmem_limit_bytes=VMEM_LIMIT)


def _pick(dim, pref):
    if dim <= pref:
        return dim
    b = pref - pref % LANE
    while b >= LANE:
        if dim % b == 0:
            return b
        b -= LANE
    return dim


def _mm_body(*refs, nk, gated, has_res, alpha):
    it = iter(refs)
    x_ref = next(it)
    w_ref = next(it)
    w2_ref = next(it) if gated else None
    res_ref = next(it) if has_res else None
    o_ref = next(it)
    acc_ref = next(it) if nk > 1 else None
    acc2_ref = next(it) if (nk > 1 and gated) else None

    def finish(p, p2):
        if gated:
            p = p * (1.0 / (1.0 + jnp.exp(-p))) * p2
        if has_res:
            p = alpha * res_ref[...] + p
        o_ref[...] = p.astype(o_ref.dtype)

    x = x_ref[...]
    p = jnp.dot(x, w_ref[...], preferred_element_type=F32)
    p2 = jnp.dot(x, w2_ref[...], preferred_element_type=F32) if gated else None
    if nk == 1:
        finish(p, p2)
        return
    k = pl.program_id(2)

    @pl.when(k == 0)
    def _():
        acc_ref[...] = p
        if gated:
            acc2_ref[...] = p2

    @pl.when(k > 0)
    def _():
        acc_ref[...] += p
        if gated:
            acc2_ref[...] += p2

    @pl.when(k == nk - 1)
    def _():
        finish(acc_ref[...], acc2_ref[...] if gated else None)


def _matmul(x, w, *, name, out_dtype, bm, bn, bk=None, w2=None, res=None, alpha=1.0):
    m, kdim = x.shape
    n = w.shape[1]
    bm = _pick(m, bm)
    bn = _pick(n, bn)
    bk = kdim if bk is None else _pick(kdim, bk)
    assert m % bm == 0 and n % bn == 0 and kdim % bk == 0
    nk = kdim // bk
    gated = w2 is not None
    has_res = res is not None
    in_specs = [pl.BlockSpec((bm, bk), lambda j, i, k: (i, k)),
                pl.BlockSpec((bk, bn), lambda j, i, k: (k, j))]
    args = [x, w]
    if gated:
        in_specs.append(pl.BlockSpec((bk, bn), lambda j, i, k: (k, j)))
        args.append(w2)
    if has_res:
        in_specs.append(pl.BlockSpec((bm, bn), lambda j, i, k: (i, j)))
        args.append(res)
    scratch = []
    if nk > 1:
        scratch.append(pltpu.VMEM((bm, bn), F32))
        if gated:
            scratch.append(pltpu.VMEM((bm, bn), F32))
    return pl.pallas_call(
        functools.partial(_mm_body, nk=nk, gated=gated, has_res=has_res, alpha=alpha),
        grid=(n // bn, m // bm, nk),
        in_specs=in_specs,
        out_specs=pl.BlockSpec((bm, bn), lambda j, i, k: (i, j)),
        out_shape=jax.ShapeDtypeStruct((m, n), out_dtype),
        scratch_shapes=scratch,
        compiler_params=_params(("parallel", "parallel", "arbitrary")),
        name=name,
    )(*args)


def _ln_body(y_ref, g_ref, b_ref, o_ref, ob_ref):
    y = y_ref[...]
    mu = jnp.mean(y, axis=-1, keepdims=True)
    d = y - mu
    var = jnp.mean(d * d, axis=-1, keepdims=True)
    o = d * lax.rsqrt(var + LN_EPS) * g_ref[...] + b_ref[...]
    o_ref[...] = o
    ob_ref[...] = o.astype(BF16)


def _layer_norm(y, g, b, *, bm=256):
    m, d = y.shape
    bm = _pick(m, bm)
    row = pl.BlockSpec((bm, d), lambda i: (i, 0))
    vec = pl.BlockSpec((1, d), lambda i: (0, 0))
    return pl.pallas_call(
        _ln_body,
        grid=(m // bm,),
        in_specs=[row, vec, vec],
        out_specs=[row, row],
        out_shape=[jax.ShapeDtypeStruct((m, d), F32), jax.ShapeDtypeStruct((m, d), BF16)],
        compiler_params=_params(("parallel",)),
        name="layer_norm",
    )(y, g.reshape(1, d), b.reshape(1, d))


def _gla_body(*refs, heads, dk, dv, L, has_init):
    it = iter(refs)
    q_ref, k_ref, v_ref, r_ref, a_ref, wa_ref, ba_ref, gn_ref = (next(it) for _ in range(8))
    s0_ref = next(it) if has_init else None
    y_ref, sout_ref, s_scr = next(it), next(it), next(it)
    c = pl.program_id(1)

    @pl.when(c == 0)
    def _():
        if has_init:
            s_scr[...] = s0_ref[...]
        else:
            s_scr[...] = jnp.zeros_like(s_scr)

    glog = jnp.dot(a_ref[...].astype(BF16), wa_ref[...], preferred_element_type=F32) + ba_ref[...]
    lg = -(jnp.maximum(-glog, 0.0) + jnp.log1p(jnp.exp(-jnp.abs(glog)))) / GLA_TAU
    rows = lax.broadcasted_iota(jnp.int32, (L, L), 0)
    cols = lax.broadcasted_iota(jnp.int32, (L, L), 1)
    causal = rows >= cols
    bcum = jnp.dot(causal.astype(F32), lg, preferred_element_type=F32, precision=lax.Precision.HIGHEST)
    btot_col = lax.dot_general(lg, jnp.ones((L, LANE), F32), (((0,), (0,)), ((), ())),
                               preferred_element_type=F32, precision=lax.Precision.HIGHEST)
    btot_row = bcum[L - 1:L, :]
    qf = q_ref[...] * jnp.exp(bcum) * (dk ** -0.5)
    kf = k_ref[...]
    k_in = kf * jnp.exp(-bcum)
    k_dec = kf * jnp.exp(btot_row - bcum)
    v = v_ref[...]
    r = r_ref[...]
    for h in range(heads):
        ks = slice(h * dk, (h + 1) * dk)
        vs = slice(h * dv, (h + 1) * dv)
        qh = qf[:, ks].astype(BF16)
        vh = v[:, vs].astype(BF16)
        att = lax.dot_general(qh, k_in[:, ks].astype(BF16), (((1,), (1,)), ((), ())), preferred_element_type=F32)
        att = jnp.where(causal, att, 0.0)
        s_old = s_scr[h]
        o = (jnp.dot(qh, s_old.astype(BF16), preferred_element_type=F32)
             + jnp.dot(att.astype(BF16), vh, preferred_element_type=F32))
        o = o * lax.rsqrt(jnp.mean(o * o, axis=-1, keepdims=True) + RMS_EPS) * gn_ref[:, vs]
        rh = r[:, vs]
        y_ref[:, vs] = (o * (rh * (1.0 / (1.0 + jnp.exp(-rh))))).astype(y_ref.dtype)
        decay = jnp.exp(btot_col[ks, :])
        upd = lax.dot_general(k_dec[:, ks].astype(BF16), vh, (((0,), (0,)), ((), ())), preferred_element_type=F32)
        s_new = jnp.concatenate([s_old[:, j * LANE:(j + 1) * LANE] * decay for j in range(dv // LANE)], axis=1) + upd
        s_scr[h] = s_new
    sout_ref[...] = s_scr[...]


def _gla(z, offs, wa, ba, gn, s0, *, nb, T, L, row0, d_gla):
    heads = GLA_HEADS
    dv = d_gla // heads
    dk = dv // 2
    kw = heads * dk
    nc = T // L
    rb0 = row0 // L
    has_init = s0 is not None
    assert row0 % L == 0 and dv % LANE == 0
    assert offs["q_g"] == 0 and offs["k_g"] == kw and offs["v_g"] == d_gla and offs["r_g"] == 2 * d_gla

    def rowblk(b, c):
        return rb0 + b * nc + c

    in_specs = [
        pl.BlockSpec((L, kw), lambda b, c: (rowblk(b, c), 0)),
        pl.BlockSpec((L, kw), lambda b, c: (rowblk(b, c), 1)),
        pl.BlockSpec((L, d_gla), lambda b, c: (rowblk(b, c), 1)),
        pl.BlockSpec((L, d_gla), lambda b, c: (rowblk(b, c), 2)),
        pl.BlockSpec((L, LANE), lambda b, c: (rowblk(b, c), offs["a_lr"] // LANE)),
        pl.BlockSpec((LANE, kw), lambda b, c: (0, 0)),
        pl.BlockSpec((1, kw), lambda b, c: (0, 0)),
        pl.BlockSpec((1, d_gla), lambda b, c: (0, 0)),
    ]
    args = [z, z, z, z, z, wa, ba, gn]
    if has_init:
        in_specs.append(pl.BlockSpec((None, heads, dk, dv), lambda b, c: (b, 0, 0, 0)))
        args.append(s0)
    return pl.pallas_call(
        functools.partial(_gla_body, heads=heads, dk=dk, dv=dv, L=L, has_init=has_init),
        grid=(nb, nc),
        in_specs=in_specs,
        out_specs=[pl.BlockSpec((L, d_gla), lambda b, c: (b * nc + c, 0)),
                   pl.BlockSpec((None, heads, dk, dv), lambda b, c: (b, 0, 0, 0))],
        out_shape=[jax.ShapeDtypeStruct((nb * T, d_gla), BF16),
                   jax.ShapeDtypeStruct((nb, heads, dk, dv), F32)],
        scratch_shapes=[pltpu.VMEM((heads, dk, dv), F32)],
        compiler_params=_params(("parallel", "arbitrary")),
        name="gla_init" if has_init else "gla",
    )(*args)


def _pool_body(*refs, Tt, pos0, has_hist, gw):
    it = iter(refs)
    u_ref = next(it)
    h_ref = next(it) if has_hist else None
    w_ref, sc_ref, y_ref, full = next(it), next(it), next(it), next(it)
    t = pl.program_id(1)
    H = POOL_HIST + 1

    @pl.when(t == 0)
    def _():
        if has_hist:
            full[0:H, :] = h_ref[...]
        else:
            full[0:H, :] = jnp.zeros((H, full.shape[1]), F32)

    @pl.when(t > 0)
    def _():
        full[0:H, :] = full[Tt:Tt + H, :]

    full[H:H + Tt, :] = u_ref[...]
    pos = pos0 + t * Tt + lax.broadcasted_iota(jnp.int32, (Tt, 1), 0)
    for g, w in enumerate(POOL_WINDOWS):
        cs = slice(g * gw, (g + 1) * gw)
        cur = full[H:H + Tt, cs]
        acc = cur
        for j in range(1, w):
            acc = acc + full[H - j:H - j + Tt, cs]
        cnt = jnp.minimum(pos + 1, w).astype(F32)
        pooled = acc / cnt - cur
        y = jnp.dot(pooled.astype(BF16), w_ref[g], preferred_element_type=F32) * sc_ref[:, cs]
        y_ref[:, cs] = y.astype(y_ref.dtype)


def _pool(z, off_u, hist, w_pool, scale, *, nb, T, row0, pos0, d_pool):
    Tt = min(T, 512)
    assert T % Tt == 0 and row0 % Tt == 0 and off_u % d_pool == 0 and Tt >= POOL_HIST + 1
    nt = T // Tt
    rb0 = row0 // Tt
    ng = len(POOL_WINDOWS)
    gw = d_pool // ng
    has_hist = hist is not None
    in_specs = [pl.BlockSpec((Tt, d_pool), lambda b, t: (rb0 + b * nt + t, off_u // d_pool))]
    args = [z]
    if has_hist:
        in_specs.append(pl.BlockSpec((None, POOL_HIST + 1, d_pool), lambda b, t: (b, 0, 0)))
        args.append(hist)
    in_specs += [pl.BlockSpec((ng, gw, gw), lambda b, t: (0, 0, 0)),
                 pl.BlockSpec((1, d_pool), lambda b, t: (0, 0))]
    args += [w_pool, scale]
    return pl.pallas_call(
        functools.partial(_pool_body, Tt=Tt, pos0=pos0, has_hist=has_hist, gw=gw),
        grid=(nb, nt),
        in_specs=in_specs,
        out_specs=pl.BlockSpec((Tt, d_pool), lambda b, t: (b * nt + t, 0)),
        out_shape=jax.ShapeDtypeStruct((nb * T, d_pool), BF16),
        scratch_shapes=[pltpu.VMEM((Tt + POOL_HIST + 1, d_pool), F32)],
        compiler_params=_params(("parallel", "arbitrary")),
        name="pool_hist" if has_hist else "pool",
    )(*args)


def _lam_init(layer):
    return 0.8 - 0.6 * math.exp(-0.3 * layer)


def _lam(lp_ref, layer):
    lp = lp_ref[...]
    a = jnp.sum(lp[0:1, :] * lp[1:2, :], axis=-1, keepdims=True)
    b = jnp.sum(lp[2:3, :] * lp[3:4, :], axis=-1, keepdims=True)
    return jnp.exp(a) - jnp.exp(b) + _lam_init(layer)


def _rope(x, cos, sin_signed):
    outs = []
    for m in range(2):
        xm = x[:, m * DIFF_DH:(m + 1) * DIFF_DH]
        outs.append(xm * cos + pltpu.roll(xm, DIFF_DH // 2, 1) * sin_signed)
    return jnp.concatenate(outs, axis=1)


def _softmax_rows(s):
    m = jnp.max(s, axis=-1, keepdims=True)
    e = jnp.exp(s - m)
    return e * (1.0 / jnp.sum(e, axis=-1, keepdims=True))


def _diff_finish(o, gn, layer):
    o = o * lax.rsqrt(jnp.mean(o * o, axis=-1, keepdims=True) + RMS_EPS) * gn
    return o * (1.0 - _lam_init(layer))


def _diffp_body(q_ref, k_ref, v_ref, cos_ref, sin_ref, lp_ref, gn_ref, y_ref, kr_ref, qb, kb, vb, *, T, bq, layer):
    dh = DIFF_DH
    lam = _lam(lp_ref, layer)
    cos = cos_ref[...]
    sin = sin_ref[...]
    kr = _rope(k_ref[...], cos, sin)
    kr_ref[...] = kr
    kb[...] = kr.astype(BF16)
    qb[...] = _rope(q_ref[...], cos, sin).astype(BF16)
    vb[...] = v_ref[...].astype(BF16)
    scale = dh ** -0.5
    for i in range(T // bq):
        n = (i + 1) * bq
        qs = slice(i * bq, n)
        qc = (i * bq + lax.broadcasted_iota(jnp.int32, (bq, n), 0)) // CHUNK
        kc = lax.broadcasted_iota(jnp.int32, (bq, n), 1) // CHUNK
        mask = kc <= qc
        ps = []
        for m in range(2):
            ms = slice(m * dh, (m + 1) * dh)
            s = lax.dot_general(qb[qs, ms], kb[0:n, ms], (((1,), (1,)), ((), ())), preferred_element_type=F32)
            ps.append(_softmax_rows(jnp.where(mask, s * scale, NEG_INF)))
        p = (ps[0] - lam * ps[1]).astype(BF16)
        o = jnp.dot(p, vb[0:n, :], preferred_element_type=F32)
        y_ref[qs, :] = _diff_finish(o, gn_ref[...], layer).astype(y_ref.dtype)


def _diff_prompt(z, offs, cos, sin, lp, gn, *, nb, T, layer, d_diff):
    hw = 2 * DIFF_DH
    heads = d_diff // hw
    bq = min(T, 256)
    assert T % bq == 0 and bq % CHUNK == 0
    for name in ("q_d", "k_d", "v_d"):
        assert offs[name] % hw == 0

    def col(name):
        return lambda b, h: (b, offs[name] // hw + h)

    tab = pl.BlockSpec((T, DIFF_DH), lambda b, h: (0, 0))
    return pl.pallas_call(
        functools.partial(_diffp_body, T=T, bq=bq, layer=layer),
        grid=(nb, heads),
        in_specs=[pl.BlockSpec((T, hw), col("q_d")), pl.BlockSpec((T, hw), col("k_d")),
                  pl.BlockSpec((T, hw), col("v_d")), tab, tab,
                  pl.BlockSpec((4, DIFF_DH), lambda b, h: (0, 0)),
                  pl.BlockSpec((1, hw), lambda b, h: (0, h))],
        out_specs=[pl.BlockSpec((T, hw), lambda b, h: (b, h)),
                   pl.BlockSpec((T, hw), lambda b, h: (b, h))],
        out_shape=[jax.ShapeDtypeStruct((nb * T, d_diff), BF16),
                   jax.ShapeDtypeStruct((nb * T, d_diff), F32)],
        scratch_shapes=[pltpu.VMEM((T, hw), BF16)] * 3,
        compiler_params=_params(("parallel", "parallel")),
        name="diff_prompt",
    )(z, z, z, cos, sin, lp, gn)


def _diffs_body(q_ref, k_ref, v_ref, pk_ref, pv_ref, cos_ref, sin_ref, lp_ref, gn_ref, y_ref, kr_ref,
                qb, s_past, s_new, p_past, acc, *, Ts, nkb, layer):
    dh = DIFF_DH
    j = pl.program_id(2)
    scale = dh ** -0.5

    @pl.when(j == 0)
    def _():
        cos = cos_ref[...]
        sin = sin_ref[...]
        kr = _rope(k_ref[...], cos, sin)
        kr_ref[...] = kr
        q = _rope(q_ref[...], cos, sin).astype(BF16)
        qb[...] = q
        for m in range(2):
            ms = slice(m * dh, (m + 1) * dh)
            s_new[m] = lax.dot_general(q[:, ms], kr[:, ms].astype(BF16), (((1,), (1,)), ((), ())),
                                       preferred_element_type=F32) * scale

    @pl.when(j < nkb)
    def _():
        pk = pk_ref[...].astype(BF16)
        for m in range(2):
            ms = slice(m * dh, (m + 1) * dh)
            s_past[m, j] = lax.dot_general(qb[:, ms], pk[:, ms], (((1,), (1,)), ((), ())),
                                           preferred_element_type=F32) * scale

    @pl.when(j == nkb)
    def _():
        lam = _lam(lp_ref, layer)
        es, en, inv = [], [], []
        for m in range(2):
            sp = s_past[m]
            sn = s_new[m]
            mx = jnp.maximum(jnp.max(jnp.max(sp, axis=0), axis=-1, keepdims=True), jnp.max(sn, axis=-1, keepdims=True))
            ep = jnp.exp(sp - mx[None])
            e_n = jnp.exp(sn - mx)
            tot = jnp.sum(jnp.sum(ep, axis=0), axis=-1, keepdims=True) + jnp.sum(e_n, axis=-1, keepdims=True)
            es.append(ep)
            en.append(e_n)
            inv.append(1.0 / tot)
        p_past[...] = (es[0] * inv[0][None] - lam * (es[1] * inv[1][None])).astype(BF16)
        pn = (en[0] * inv[0] - lam * (en[1] * inv[1])).astype(BF16)
        acc[...] = jnp.dot(pn, v_ref[...].astype(BF16), preferred_element_type=F32)

    @pl.when(j >= nkb)
    def _():
        acc[...] += jnp.dot(p_past[j - nkb], pv_ref[...].astype(BF16), preferred_element_type=F32)

    @pl.when(j == 2 * nkb - 1)
    def _():
        y_ref[...] = _diff_finish(acc[...], gn_ref[...], layer).astype(y_ref.dtype)


def _diff_sample(z, offs, past_k, past_v, cos, sin, lp, gn, *, nb, Ts, row0, layer, d_diff):
    hw = 2 * DIFF_DH
    heads = d_diff // hw
    P = past_k.shape[2]
    kb = _pick(P, 1024)
    nkb = P // kb
    rb0 = row0 // Ts
    assert row0 % Ts == 0 and P % kb == 0

    def col(name):
        return lambda b, h, j: (rb0 + b, offs[name] // hw + h)

    tab = pl.BlockSpec((Ts, DIFF_DH), lambda b, h, j: (0, 0))
    return pl.pallas_call(
        functools.partial(_diffs_body, Ts=Ts, nkb=nkb, layer=layer),
        grid=(nb, heads, 2 * nkb),
        in_specs=[pl.BlockSpec((Ts, hw), col("q_d")), pl.BlockSpec((Ts, hw), col("k_d")),
                  pl.BlockSpec((Ts, hw), col("v_d")),
                  pl.BlockSpec((None, None, kb, hw), lambda b, h, j: (layer, b, jnp.minimum(j, nkb - 1), h)),
                  pl.BlockSpec((None, None, kb, hw), lambda b, h, j: (layer, b, jnp.maximum(j - nkb, 0), h)),
                  tab, tab,
                  pl.BlockSpec((4, DIFF_DH), lambda b, h, j: (0, 0)),
                  pl.BlockSpec((1, hw), lambda b, h, j: (0, h))],
        out_specs=[pl.BlockSpec((Ts, hw), lambda b, h, j: (b, h)),
                   pl.BlockSpec((Ts, hw), lambda b, h, j: (b, h))],
        out_shape=[jax.ShapeDtypeStruct((nb * Ts, d_diff), BF16),
                   jax.ShapeDtypeStruct((nb * Ts, d_diff), F32)],
        scratch_shapes=[pltpu.VMEM((Ts, hw), BF16),
                        pltpu.VMEM((2, nkb, Ts, kb), F32),
                        pltpu.VMEM((2, Ts, Ts), F32),
                        pltpu.VMEM((nkb, Ts, kb), BF16),
                        pltpu.VMEM((Ts, hw), F32)],
        compiler_params=_params(("parallel", "parallel", "arbitrary")),
        name="diff_sample",
    )(z, z, z, past_k, past_v, cos, sin, lp, gn)


def _mem_body(q_ref, k_ref, v_ref, o_ref, *, heads, dh):
    scale = dh ** -0.5
    for h in range(heads):
        hs = slice(h * dh, (h + 1) * dh)
        s = lax.dot_general(q_ref[:, hs], k_ref[:, hs].astype(BF16), (((1,), (1,)), ((), ())),
                            preferred_element_type=F32) * scale
        p = _softmax_rows(s).astype(BF16)
        o_ref[:, hs] = jnp.dot(p, v_ref[:, hs].astype(BF16), preferred_element_type=F32).astype(o_ref.dtype)


def _mem_attn(q, mem_k, mem_v, *, layer, nb, T, row0, heads):
    d = q.shape[1]
    n_mem = mem_k.shape[2]
    bt = min(T, 512)
    nt = T // bt
    rb0 = row0 // bt
    assert T % bt == 0 and row0 % bt == 0
    kv = pl.BlockSpec((None, None, n_mem, d), lambda b, t: (layer, b, 0, 0))
    return pl.pallas_call(
        functools.partial(_mem_body, heads=heads, dh=d // heads),
        grid=(nb, nt),
        in_specs=[pl.BlockSpec((bt, d), lambda b, t: (rb0 + b * nt + t, 0)), kv, kv],
        out_specs=pl.BlockSpec((bt, d), lambda b, t: (b * nt + t, 0)),
        out_shape=jax.ShapeDtypeStruct((nb * T, d), BF16),
        compiler_params=_params(("parallel", "parallel")),
        name="mem_attn",
    )(q, mem_k, mem_v)


def _rope_tables(pos):
    inv = ROPE_THETA ** (-jnp.arange(0, DIFF_DH, 2, dtype=F32) / DIFF_DH)
    ang = pos.astype(F32)[:, None] * inv[None, :]
    cos, sin = jnp.cos(ang), jnp.sin(ang)
    return jnp.concatenate([cos, cos], axis=1), jnp.concatenate([-sin, sin], axis=1)


def kernel(x_prompt, x_sample, cache_diff_k, cache_diff_v, state_gla, state_pool, cache_mem_k, cache_mem_v,
           mem_prompt, w_in, gla_w_a, gla_b_a, gla_norm, pool_w, pool_scale, diff_lambda, diff_norm, w_out,
           w_mq, w_mk, w_mv, w_mo, w_gate, w_up, w_down, ln_g, ln_b):
    Bp, Tp, D = x_prompt.shape
    Bs, Ts, _ = x_sample.shape
    depth = w_in.shape[0]
    past_len = cache_diff_k.shape[2]
    n_mem = mem_prompt.shape[1]
    mem_heads = cache_mem_k.shape[3]
    d_gla = gla_norm.shape[1]
    d_pool = pool_scale.shape[1]
    d_diff = diff_norm.shape[1]
    kw = gla_w_a.shape[2]
    lr = gla_w_a.shape[1]
    d_ff = w_gate.shape[2]
    alpha = (2 * depth) ** 0.25
    Mp, Ms = Bp * Tp, Bs * Ts
    assert lr <= LANE

    n_head = 2 * kw + 2 * d_gla
    n_main = n_head + 3 * d_diff + d_pool
    offs = {"q_g": 0, "k_g": kw, "v_g": 2 * kw, "r_g": 2 * kw + d_gla,
            "q_d": n_head, "k_d": n_head + d_diff, "v_d": n_head + 2 * d_diff, "u_p": n_head + 3 * d_diff,
            "a_lr": n_main}
    d_ff_pad = -(-d_ff // 1024) * 1024

    x = jnp.concatenate([x_prompt.reshape(Mp, D), x_sample.reshape(Ms, D)], axis=0)
    xb = x.astype(BF16)
    memb = mem_prompt.reshape(Bp * n_mem, D).astype(BF16)
    cos_p, sin_p = _rope_tables(jnp.arange(Tp))
    cos_s, sin_s = _rope_tables(past_len + jnp.arange(Ts))
    past_k = cache_diff_k.reshape(depth, Bs, past_len, d_diff)
    past_v = cache_diff_v.reshape(depth, Bs, past_len, d_diff)
    cmem_k = cache_mem_k.reshape(depth, Bs, n_mem, D)
    cmem_v = cache_mem_v.reshape(depth, Bs, n_mem, D)
    hist_s = jnp.pad(state_pool, ((0, 0), (0, 0), (1, 0), (0, 0)))

    outs = {k: [] for k in ("kp", "vp", "Sp", "hp", "mk", "mv", "ks", "vs", "Ss", "hs")}
    for l in range(depth):
        wi = w_in[l]
        w_in_r = jnp.concatenate([wi[:, :n_head], wi[:, n_head + lr:], wi[:, n_head:n_head + lr],
                                  jnp.zeros((D, LANE - lr), F32)], axis=1).astype(BF16)
        wa = jnp.pad(gla_w_a[l], ((0, LANE - lr), (0, 0))).astype(BF16)
        ba = gla_b_a[l].reshape(1, kw)
        gn_g = gla_norm[l].reshape(1, d_gla)
        gn_d = diff_norm[l].reshape(1, d_diff)
        pw = pool_w[l].astype(BF16)
        psc = pool_scale[l].reshape(1, d_pool)
        pad_ff = ((0, 0), (0, d_ff_pad - d_ff))
        wg = jnp.pad(w_gate[l].astype(BF16), pad_ff)
        wu = jnp.pad(w_up[l].astype(BF16), pad_ff)
        wd = jnp.pad(w_down[l].astype(BF16), ((0, d_ff_pad - d_ff), (0, 0)))

        z = _matmul(xb, w_in_r, name="mm_in", out_dtype=F32, bm=1024, bn=1152)
        yg_p, S_p = _gla(z, offs, wa, ba, gn_g, None, nb=Bp, T=Tp, L=CHUNK, row0=0, d_gla=d_gla)
        yg_s, S_s = _gla(z, offs, wa, ba, gn_g, state_gla[l], nb=Bs, T=Ts, L=Ts, row0=Mp, d_gla=d_gla)
        yp_p = _pool(z, offs["u_p"], None, pw, psc, nb=Bp, T=Tp, row0=0, pos0=0, d_pool=d_pool)
        yp_s = _pool(z, offs["u_p"], hist_s[l], pw, psc, nb=Bs, T=Ts, row0=Mp, pos0=past_len, d_pool=d_pool)
        yd_p, kr_p = _diff_prompt(z, offs, cos_p, sin_p, diff_lambda[l], gn_d, nb=Bp, T=Tp, layer=l, d_diff=d_diff)
        yd_s, kr_s = _diff_sample(z, offs, past_k, past_v, cos_s, sin_s, diff_lambda[l], gn_d,
                                  nb=Bs, Ts=Ts, row0=Mp, layer=l, d_diff=d_diff)
        ymix = jnp.concatenate([jnp.concatenate([yg_p, yp_p, yd_p], axis=1),
                                jnp.concatenate([yg_s, yp_s, yd_s], axis=1)], axis=0)
        y = _matmul(ymix, w_out[l].astype(BF16), name="mm_out", out_dtype=F32, bm=1024, bn=1024, res=x, alpha=alpha)
        x, xb = _layer_norm(y, ln_g[l, 0], ln_b[l, 0])

        mk_p = _matmul(memb, w_mk[l].astype(BF16), name="mm_mk", out_dtype=F32, bm=1024, bn=1024)
        mv_p = _matmul(memb, w_mv[l].astype(BF16), name="mm_mv", out_dtype=F32, bm=1024, bn=1024)
        q = _matmul(xb, w_mq[l].astype(BF16), name="mm_mq", out_dtype=BF16, bm=1024, bn=1024)
        o_p = _mem_attn(q, mk_p.reshape(1, Bp, n_mem, D), mv_p.reshape(1, Bp, n_mem, D),
                        layer=0, nb=Bp, T=Tp, row0=0, heads=mem_heads)
        o_s = _mem_attn(q, cmem_k, cmem_v, layer=l, nb=Bs, T=Ts, row0=Mp, heads=mem_heads)
        o = jnp.concatenate([o_p, o_s], axis=0)
        y = _matmul(o, w_mo[l].astype(BF16), name="mm_mo", out_dtype=F32, bm=1024, bn=1024, res=x, alpha=alpha)
        x, xb = _layer_norm(y, ln_g[l, 1], ln_b[l, 1])

        hmid = _matmul(xb, wg, w2=wu, name="mm_gate_up", out_dtype=BF16, bm=1024, bn=512)
        y = _matmul(hmid, wd, name="mm_down", out_dtype=F32, bm=1024, bn=1024, bk=d_ff_pad // 4, res=x, alpha=alpha)
        x, xb = _layer_norm(y, ln_g[l, 2], ln_b[l, 2])

        hw = 2 * DIFF_DH
        nh = d_diff // hw
        v_all = z[:, offs["v_d"]:offs["v_d"] + d_diff]
        u_all = z[:, offs["u_p"]:offs["u_p"] + d_pool]
        outs["kp"].append(kr_p.reshape(Bp, Tp, nh, 2, DIFF_DH))
        outs["vp"].append(v_all[:Mp].reshape(Bp, Tp, nh, hw))
        outs["Sp"].append(S_p)
        outs["hp"].append(u_all[:Mp].reshape(Bp, Tp, d_pool)[:, Tp - POOL_HIST:])
        outs["mk"].append(mk_p.reshape(Bp, n_mem, mem_heads, D // mem_heads))
        outs["mv"].append(mv_p.reshape(Bp, n_mem, mem_heads, D // mem_heads))
        outs["ks"].append(kr_s.reshape(Bs, Ts, nh, 2, DIFF_DH))
        outs["vs"].append(v_all[Mp:].reshape(Bs, Ts, nh, hw))
        outs["Ss"].append(S_s)
        outs["hs"].append(u_all[Mp:].reshape(Bs, Ts, d_pool)[:, Ts - POOL_HIST:])

    st = {k: jnp.stack(v) for k, v in outs.items()}
    return (x[:Mp].reshape(Bp, Tp, D), x[Mp:].reshape(Bs, Ts, D),
            st["kp"], st["vp"], st["Sp"], st["hp"], st["mk"], st["mv"],
            st["ks"], st["vs"], st["Ss"], st["hs"])
```

```python
import functools
import math

import jax
import jax.numpy as jnp
from jax import lax
from jax.experimental import pallas as pl
from jax.experimental.pallas import tpu as pltpu

F32 = jnp.float32
BF16 = jnp.bfloat16

LANE = 128
V7X_VMEM_BYTES = 64 * 1024 * 1024
VMEM_LIMIT = V7X_VMEM_BYTES - 8 * 1024 * 1024

CHUNK = 64
GLA_HEADS = 4
GLA_TAU = 16.0
POOL_WINDOWS = (2, 4, 8, 16)
POOL_HIST = max(POOL_WINDOWS) - 1
DIFF_DH = 128
ROPE_THETA = 10000.0
NEG_INF = -1e30
LN_EPS = 1e-5
RMS_EPS = 1e-6


def _params(sem):
    return pltpu.CompilerParams(dimension_semantics=sem, vmem_limit_bytes=VMEM_LIMIT)


def _pick(dim, pref):
    if dim <= pref:
        return dim
    b = pref - pref % LANE
    while b >= LANE:
        if dim % b == 0:
            return b
        b -= LANE
    return dim


def _mm_body(*refs, nk, gated, has_res, alpha):
    it = iter(refs)
    x_ref = next(it)
    w_ref = next(it)
    w2_ref = next(it) if gated else None
    res_ref = next(it) if has_res else None
    o_ref = next(it)
    acc_ref = next(it) if nk > 1 else None
    acc2_ref = next(it) if (nk > 1 and gated) else None

    def finish(p, p2):
        if gated:
            p = p * (1.0 / (1.0 + jnp.exp(-p))) * p2
        if has_res:
            p = alpha * res_ref[...] + p
        o_ref[...] = p.astype(o_ref.dtype)

    x = x_ref[...]
    p = jnp.dot(x, w_ref[...], preferred_element_type=F32)
    p2 = jnp.dot(x, w2_ref[...], preferred_element_type=F32) if gated else None
    if nk == 1:
        finish(p, p2)
        return
    k = pl.program_id(2)

    @pl.when(k == 0)
    def _():
        acc_ref[...] = p
        if gated:
            acc2_ref[...] = p2

    @pl.when(k > 0)
    def _():
        acc_ref[...] += p
        if gated:
            acc2_ref[...] += p2

    @pl.when(k == nk - 1)
    def _():
        finish(acc_ref[...], acc2_ref[...] if gated else None)


def _matmul(x, w, *, name, out_dtype, bm, bn, bk=None, w2=None, res=None, alpha=1.0):
    m, kdim = x.shape
    n = w.shape[1]
    bm = _pick(m, bm)
    bn = _pick(n, bn)
    bk = kdim if bk is None else _pick(kdim, bk)
    assert m % bm == 0 and n % bn == 0 and kdim % bk == 0
    nk = kdim // bk
    gated = w2 is not None
    has_res = res is not None
    in_specs = [pl.BlockSpec((bm, bk), lambda j, i, k: (i, k)),
                pl.BlockSpec((bk, bn), lambda j, i, k: (k, j))]
    args = [x, w]
    if gated:
        in_specs.append(pl.BlockSpec((bk, bn), lambda j, i, k: (k, j)))
        args.append(w2)
    if has_res:
        in_specs.append(pl.BlockSpec((bm, bn), lambda j, i, k: (i, j)))
        args.append(res)
    scratch = []
    if nk > 1:
        scratch.append(pltpu.VMEM((bm, bn), F32))
        if gated:
            scratch.append(pltpu.VMEM((bm, bn), F32))
    return pl.pallas_call(
        functools.partial(_mm_body, nk=nk, gated=gated, has_res=has_res, alpha=alpha),
        grid=(n // bn, m // bm, nk),
        in_specs=in_specs,
        out_specs=pl.BlockSpec((bm, bn), lambda j, i, k: (i, j)),
        out_shape=jax.ShapeDtypeStruct((m, n), out_dtype),
        scratch_shapes=scratch,
        compiler_params=_params(("parallel", "parallel", "arbitrary")),
        name=name,
    )(*args)


def _mm_split_body(*refs, widths, nbp, alpha):
    npc = len(widths)
    p_refs, s_refs = refs[:npc], refs[npc:2 * npc]
    w_ref, res_ref, o_ref = refs[2 * npc:2 * npc + 3]
    i = pl.program_id(1)

    def run(lhs_refs):
        acc, k0 = None, 0
        for r, wd in zip(lhs_refs, widths):
            d = jnp.dot(r[...], w_ref[k0:k0 + wd, :], preferred_element_type=F32)
            acc = d if acc is None else acc + d
            k0 += wd
        o_ref[...] = (alpha * res_ref[...] + acc).astype(o_ref.dtype)

    @pl.when(i < nbp)
    def _():
        run(p_refs)

    @pl.when(i >= nbp)
    def _():
        run(s_refs)


def _matmul_split(pieces, w, res, *, name, alpha, bm, bn):
    mp, ms = pieces[0][0].shape[0], pieces[0][1].shape[0]
    widths = tuple(p.shape[1] for p, _ in pieces)
    kdim, n = w.shape
    assert sum(widths) == kdim and res.shape == (mp + ms, n)
    bm = math.gcd(math.gcd(mp, ms), bm)
    bn = _pick(n, bn)
    nbp = mp // bm
    p_specs = [pl.BlockSpec((bm, wd), lambda j, i: (jnp.minimum(i, nbp - 1), 0)) for wd in widths]
    s_specs = [pl.BlockSpec((bm, wd), lambda j, i: (jnp.maximum(i - nbp, 0), 0)) for wd in widths]
    return pl.pallas_call(
        functools.partial(_mm_split_body, widths=widths, nbp=nbp, alpha=alpha),
        grid=(n // bn, (mp + ms) // bm),
        in_specs=p_specs + s_specs + [pl.BlockSpec((kdim, bn), lambda j, i: (0, j)),
                                      pl.BlockSpec((bm, bn), lambda j, i: (i, j))],
        out_specs=pl.BlockSpec((bm, bn), lambda j, i: (i, j)),
        out_shape=jax.ShapeDtypeStruct((mp + ms, n), F32),
        compiler_params=_params(("parallel", "arbitrary")),
        name=name,
    )(*[p for p, _ in pieces], *[s for _, s in pieces], w, res)


def _ln_body(y_ref, g_ref, b_ref, o_ref, ob_ref):
    y = y_ref[...]
    mu = jnp.mean(y, axis=-1, keepdims=True)
    d = y - mu
    var = jnp.mean(d * d, axis=-1, keepdims=True)
    o = d * lax.rsqrt(var + LN_EPS) * g_ref[...] + b_ref[...]
    o_ref[...] = o
    ob_ref[...] = o.astype(BF16)


def _layer_norm(y, g, b, *, bm=256):
    m, d = y.shape
    bm = _pick(m, bm)
    row = pl.BlockSpec((bm, d), lambda i: (i, 0))
    vec = pl.BlockSpec((1, d), lambda i: (0, 0))
    return pl.pallas_call(
        _ln_body,
        grid=(m // bm,),
        in_specs=[row, vec, vec],
        out_specs=[row, row],
        out_shape=[jax.ShapeDtypeStruct((m, d), F32), jax.ShapeDtypeStruct((m, d), BF16)],
        compiler_params=_params(("parallel",)),
        name="layer_norm",
    )(y, g.reshape(1, d), b.reshape(1, d))


def _gla_body(*refs, heads, dk, dv, L, has_init):
    it = iter(refs)
    q_ref, k_ref, v_ref, r_ref, a_ref, wa_ref, ba_ref, gn_ref = (next(it) for _ in range(8))
    s0_ref = next(it) if has_init else None
    y_ref, sout_ref, s_scr = next(it), next(it), next(it)
    c = pl.program_id(1)

    @pl.when(c == 0)
    def _():
        if has_init:
            s_scr[...] = s0_ref[...]
        else:
            s_scr[...] = jnp.zeros_like(s_scr)

    glog = jnp.dot(a_ref[...].astype(BF16), wa_ref[...], preferred_element_type=F32) + ba_ref[...]
    lg = -(jnp.maximum(-glog, 0.0) + jnp.log1p(jnp.exp(-jnp.abs(glog)))) / GLA_TAU
    rows = lax.broadcasted_iota(jnp.int32, (L, L), 0)
    cols = lax.broadcasted_iota(jnp.int32, (L, L), 1)
    causal = rows >= cols
    bcum = jnp.dot(causal.astype(F32), lg, preferred_element_type=F32, precision=lax.Precision.HIGHEST)
    btot_col = lax.dot_general(lg, jnp.ones((L, LANE), F32), (((0,), (0,)), ((), ())),
                               preferred_element_type=F32, precision=lax.Precision.HIGHEST)
    btot_row = bcum[L - 1:L, :]
    qf = q_ref[...] * jnp.exp(bcum) * (dk ** -0.5)
    kf = k_ref[...]
    k_in = kf * jnp.exp(-bcum)
    k_dec = kf * jnp.exp(btot_row - bcum)
    v = v_ref[...]
    r = r_ref[...]
    for h in range(heads):
        ks = slice(h * dk, (h + 1) * dk)
        vs = slice(h * dv, (h + 1) * dv)
        qh = qf[:, ks].astype(BF16)
        vh = v[:, vs].astype(BF16)
        att = lax.dot_general(qh, k_in[:, ks].astype(BF16), (((1,), (1,)), ((), ())), preferred_element_type=F32)
        att = jnp.where(causal, att, 0.0)
        s_old = s_scr[h]
        o = (jnp.dot(qh, s_old.astype(BF16), preferred_element_type=F32)
             + jnp.dot(att.astype(BF16), vh, preferred_element_type=F32))
        o = o * lax.rsqrt(jnp.mean(o * o, axis=-1, keepdims=True) + RMS_EPS) * gn_ref[:, vs]
        rh = r[:, vs]
        y_ref[:, vs] = (o * (rh * (1.0 / (1.0 + jnp.exp(-rh))))).astype(y_ref.dtype)
        decay = jnp.exp(btot_col[ks, :])
        upd = lax.dot_general(k_dec[:, ks].astype(BF16), vh, (((0,), (0,)), ((), ())), preferred_element_type=F32)
        s_new = jnp.concatenate([s_old[:, j * LANE:(j + 1) * LANE] * decay for j in range(dv // LANE)], axis=1) + upd
        s_scr[h] = s_new
    sout_ref[...] = s_scr[...]


def _gla(z, offs, wa, ba, gn, s0, *, nb, T, L, row0, d_gla):
    heads = GLA_HEADS
    dv = d_gla // heads
    dk = dv // 2
    kw = heads * dk
    nc = T // L
    rb0 = row0 // L
    has_init = s0 is not None
    assert row0 % L == 0 and dv % LANE == 0
    assert offs["q_g"] == 0 and offs["k_g"] == kw and offs["v_g"] == d_gla and offs["r_g"] == 2 * d_gla

    def rowblk(b, c):
        return rb0 + b * nc + c

    in_specs = [
        pl.BlockSpec((L, kw), lambda b, c: (rowblk(b, c), 0)),
        pl.BlockSpec((L, kw), lambda b, c: (rowblk(b, c), 1)),
        pl.BlockSpec((L, d_gla), lambda b, c: (rowblk(b, c), 1)),
        pl.BlockSpec((L, d_gla), lambda b, c: (rowblk(b, c), 2)),
        pl.BlockSpec((L, LANE), lambda b, c: (rowblk(b, c), offs["a_lr"] // LANE)),
        pl.BlockSpec((LANE, kw), lambda b, c: (0, 0)),
        pl.BlockSpec((1, kw), lambda b, c: (0, 0)),
        pl.BlockSpec((1, d_gla), lambda b, c: (0, 0)),
    ]
    args = [z, z, z, z, z, wa, ba, gn]
    if has_init:
        in_specs.append(pl.BlockSpec((None, heads, dk, dv), lambda b, c: (b, 0, 0, 0)))
        args.append(s0)
    return pl.pallas_call(
        functools.partial(_gla_body, heads=heads, dk=dk, dv=dv, L=L, has_init=has_init),
        grid=(nb, nc),
        in_specs=in_specs,
        out_specs=[pl.BlockSpec((L, d_gla), lambda b, c: (b * nc + c, 0)),
                   pl.BlockSpec((None, heads, dk, dv), lambda b, c: (b, 0, 0, 0))],
        out_shape=[jax.ShapeDtypeStruct((nb * T, d_gla), BF16),
                   jax.ShapeDtypeStruct((nb, heads, dk, dv), F32)],
        scratch_shapes=[pltpu.VMEM((heads, dk, dv), F32)],
        compiler_params=_params(("parallel", "arbitrary")),
        name="gla_init" if has_init else "gla",
    )(*args)


def _pool_body(*refs, Tt, pos0, has_hist, gw):
    it = iter(refs)
    u_ref = next(it)
    h_ref = next(it) if has_hist else None
    w_ref, sc_ref, y_ref, full = next(it), next(it), next(it), next(it)
    t = pl.program_id(1)
    H = POOL_HIST + 1

    @pl.when(t == 0)
    def _():
        if has_hist:
            full[0:H, :] = h_ref[...]
        else:
            full[0:H, :] = jnp.zeros((H, full.shape[1]), F32)

    @pl.when(t > 0)
    def _():
        full[0:H, :] = full[Tt:Tt + H, :]

    full[H:H + Tt, :] = u_ref[...]
    pos = pos0 + t * Tt + lax.broadcasted_iota(jnp.int32, (Tt, 1), 0)
    for g, w in enumerate(POOL_WINDOWS):
        cs = slice(g * gw, (g + 1) * gw)
        cur = full[H:H + Tt, cs]
        acc = cur
        for j in range(1, w):
            acc = acc + full[H - j:H - j + Tt, cs]
        cnt = jnp.minimum(pos + 1, w).astype(F32)
        pooled = acc / cnt - cur
        y = jnp.dot(pooled.astype(BF16), w_ref[g], preferred_element_type=F32) * sc_ref[:, cs]
        y_ref[:, cs] = y.astype(y_ref.dtype)


def _pool(z, off_u, hist, w_pool, scale, *, nb, T, row0, pos0, d_pool):
    Tt = min(T, 512)
    assert T % Tt == 0 and row0 % Tt == 0 and off_u % d_pool == 0 and Tt >= POOL_HIST + 1
    nt = T // Tt
    rb0 = row0 // Tt
    ng = len(POOL_WINDOWS)
    gw = d_pool // ng
    has_hist = hist is not None
    in_specs = [pl.BlockSpec((Tt, d_pool), lambda b, t: (rb0 + b * nt + t, off_u // d_pool))]
    args = [z]
    if has_hist:
        in_specs.append(pl.BlockSpec((None, POOL_HIST + 1, d_pool), lambda b, t: (b, 0, 0)))
        args.append(hist)
    in_specs += [pl.BlockSpec((ng, gw, gw), lambda b, t: (0, 0, 0)),
                 pl.BlockSpec((1, d_pool), lambda b, t: (0, 0))]
    args += [w_pool, scale]
    return pl.pallas_call(
        functools.partial(_pool_body, Tt=Tt, pos0=pos0, has_hist=has_hist, gw=gw),
        grid=(nb, nt),
        in_specs=in_specs,
        out_specs=pl.BlockSpec((Tt, d_pool), lambda b, t: (b * nt + t, 0)),
        out_shape=jax.ShapeDtypeStruct((nb * T, d_pool), BF16),
        scratch_shapes=[pltpu.VMEM((Tt + POOL_HIST + 1, d_pool), F32)],
        compiler_params=_params(("parallel", "arbitrary")),
        name="pool_hist" if has_hist else "pool",
    )(*args)


def _lam_init(layer):
    return 0.8 - 0.6 * math.exp(-0.3 * layer)


def _lam(lp_ref, layer):
    lp = lp_ref[...]
    a = jnp.sum(lp[0:1, :] * lp[1:2, :], axis=-1, keepdims=True)
    b = jnp.sum(lp[2:3, :] * lp[3:4, :], axis=-1, keepdims=True)
    return jnp.exp(a) - jnp.exp(b) + _lam_init(layer)


def _rope(x, cos, sin_signed):
    outs = []
    for m in range(2):
        xm = x[:, m * DIFF_DH:(m + 1) * DIFF_DH]
        outs.append(xm * cos + pltpu.roll(xm, DIFF_DH // 2, 1) * sin_signed)
    return jnp.concatenate(outs, axis=1)


def _softmax_rows(s):
    m = jnp.max(s, axis=-1, keepdims=True)
    e = jnp.exp(s - m)
    return e * (1.0 / jnp.sum(e, axis=-1, keepdims=True))


def _diff_finish(o, gn, layer):
    o = o * lax.rsqrt(jnp.mean(o * o, axis=-1, keepdims=True) + RMS_EPS) * gn
    return o * (1.0 - _lam_init(layer))


def _diffp_body(q_ref, k_ref, v_ref, cos_ref, sin_ref, lp_ref, gn_ref, y_ref, kr_ref, qb, kb, vb, *, T, bq, layer):
    dh = DIFF_DH
    lam = _lam(lp_ref, layer)
    cos = cos_ref[...]
    sin = sin_ref[...]
    kr = _rope(k_ref[...], cos, sin)
    kr_ref[...] = kr
    kb[...] = kr.astype(BF16)
    qb[...] = _rope(q_ref[...], cos, sin).astype(BF16)
    vb[...] = v_ref[...].astype(BF16)
    scale = dh ** -0.5
    for i in range(T // bq):
        n = (i + 1) * bq
        qs = slice(i * bq, n)
        qc = (i * bq + lax.broadcasted_iota(jnp.int32, (bq, n), 0)) // CHUNK
        kc = lax.broadcasted_iota(jnp.int32, (bq, n), 1) // CHUNK
        mask = kc <= qc
        ps = []
        for m in range(2):
            ms = slice(m * dh, (m + 1) * dh)
            s = lax.dot_general(qb[qs, ms], kb[0:n, ms], (((1,), (1,)), ((), ())), preferred_element_type=F32)
            ps.append(_softmax_rows(jnp.where(mask, s * scale, NEG_INF)))
        p = (ps[0] - lam * ps[1]).astype(BF16)
        o = jnp.dot(p, vb[0:n, :], preferred_element_type=F32)
        y_ref[qs, :] = _diff_finish(o, gn_ref[...], layer).astype(y_ref.dtype)


def _diff_prompt(z, offs, cos, sin, lp, gn, *, nb, T, layer, d_diff):
    hw = 2 * DIFF_DH
    heads = d_diff // hw
    bq = min(T, 256)
    assert T % bq == 0 and bq % CHUNK == 0
    for name in ("q_d", "k_d", "v_d"):
        assert offs[name] % hw == 0

    def col(name):
        return lambda b, h: (b, offs[name] // hw + h)

    tab = pl.BlockSpec((T, DIFF_DH), lambda b, h: (0, 0))
    return pl.pallas_call(
        functools.partial(_diffp_body, T=T, bq=bq, layer=layer),
        grid=(nb, heads),
        in_specs=[pl.BlockSpec((T, hw), col("q_d")), pl.BlockSpec((T, hw), col("k_d")),
                  pl.BlockSpec((T, hw), col("v_d")), tab, tab,
                  pl.BlockSpec((4, DIFF_DH), lambda b, h: (0, 0)),
                  pl.BlockSpec((1, hw), lambda b, h: (0, h))],
        out_specs=[pl.BlockSpec((T, hw), lambda b, h: (b, h)),
                   pl.BlockSpec((T, hw), lambda b, h: (b, h))],
        out_shape=[jax.ShapeDtypeStruct((nb * T, d_diff), BF16),
                   jax.ShapeDtypeStruct((nb * T, d_diff), F32)],
        scratch_shapes=[pltpu.VMEM((T, hw), BF16)] * 3,
        compiler_params=_params(("parallel", "parallel")),
        name="diff_prompt",
    )(z, z, z, cos, sin, lp, gn)


def _diffs_body(q_ref, k_ref, v_ref, pk_ref, pv_ref, cos_ref, sin_ref, lp_ref, gn_ref, y_ref, kr_ref,
                qb, s_past, s_new, p_past, acc, *, Ts, kb, nkb, heads, layer):
    dh = DIFF_DH
    hw = 2 * dh
    nmap = 2 * heads
    j = pl.program_id(1)
    scale = dh ** -0.5

    @pl.when(j == 0)
    def _():
        cos = cos_ref[...]
        sin = sin_ref[...]
        for c in range(nmap):
            cs = slice(c * dh, (c + 1) * dh)
            kc = k_ref[:, cs]
            kc = kc * cos + pltpu.roll(kc, dh // 2, 1) * sin
            kr_ref[pl.ds(c, Ts, stride=nmap), :] = kc
            qc = q_ref[:, cs]
            qc = (qc * cos + pltpu.roll(qc, dh // 2, 1) * sin).astype(BF16)
            qb[:, cs] = qc
            s_new[c] = lax.dot_general(qc, kc.astype(BF16), (((1,), (1,)), ((), ())),
                                       preferred_element_type=F32) * scale

    @pl.when(j < nkb)
    def _():
        for c in range(nmap):
            pk = pk_ref[pl.ds(c, kb, stride=nmap), :].astype(BF16)
            s_past[c, j] = lax.dot_general(qb[:, c * dh:(c + 1) * dh], pk, (((1,), (1,)), ((), ())),
                                           preferred_element_type=F32) * scale

    @pl.when(j == nkb)
    def _():
        lam = _lam(lp_ref, layer)
        for h in range(heads):
            es, en, inv = [], [], []
            for m in range(2):
                sp = s_past[2 * h + m]
                sn = s_new[2 * h + m]
                mx = jnp.maximum(jnp.max(jnp.max(sp, axis=0), axis=-1, keepdims=True),
                                 jnp.max(sn, axis=-1, keepdims=True))
                ep = jnp.exp(sp - mx[None])
                e_n = jnp.exp(sn - mx)
                tot = jnp.sum(jnp.sum(ep, axis=0), axis=-1, keepdims=True) + jnp.sum(e_n, axis=-1, keepdims=True)
                es.append(ep)
                en.append(e_n)
                inv.append(1.0 / tot)
            p_past[h] = (es[0] * inv[0][None] - lam * (es[1] * inv[1][None])).astype(BF16)
            pn = (en[0] * inv[0] - lam * (en[1] * inv[1])).astype(BF16)
            hs = slice(h * hw, (h + 1) * hw)
            acc[:, hs] = jnp.dot(pn, v_ref[:, hs].astype(BF16), preferred_element_type=F32)

    @pl.when(j >= nkb)
    def _():
        for h in range(heads):
            hs = slice(h * hw, (h + 1) * hw)
            acc[:, hs] += jnp.dot(p_past[h, j - nkb], pv_ref[h].astype(BF16), preferred_element_type=F32)

    @pl.when(j == 2 * nkb - 1)
    def _():
        for h in range(heads):
            hs = slice(h * hw, (h + 1) * hw)
            y_ref[:, hs] = _diff_finish(acc[:, hs], gn_ref[:, hs], layer).astype(y_ref.dtype)


def _diff_sample(z, offs, past_k, past_v, cos, sin, lp, gn, *, nb, Ts, row0, layer, d_diff):
    hw = 2 * DIFF_DH
    heads = d_diff // hw
    nmap = 2 * heads
    P = past_v.shape[3]
    kb = _pick(P, 1024)
    nkb = P // kb
    rb0 = row0 // Ts
    assert row0 % Ts == 0 and P % kb == 0
    for name in ("q_d", "k_d", "v_d"):
        assert offs[name] % d_diff == 0

    def col(name):
        return lambda b, j: (rb0 + b, offs[name] // d_diff)

    tab = pl.BlockSpec((Ts, DIFF_DH), lambda b, j: (0, 0))
    return pl.pallas_call(
        functools.partial(_diffs_body, Ts=Ts, kb=kb, nkb=nkb, heads=heads, layer=layer),
        grid=(nb, 2 * nkb),
        in_specs=[pl.BlockSpec((Ts, d_diff), col("q_d")), pl.BlockSpec((Ts, d_diff), col("k_d")),
                  pl.BlockSpec((Ts, d_diff), col("v_d")),
                  pl.BlockSpec((None, None, kb * nmap, DIFF_DH), lambda b, j: (layer, b, jnp.minimum(j, nkb - 1), 0)),
                  pl.BlockSpec((None, None, heads, kb, hw), lambda b, j: (layer, b, 0, jnp.maximum(j - nkb, 0), 0)),
                  tab, tab,
                  pl.BlockSpec((4, DIFF_DH), lambda b, j: (0, 0)),
                  pl.BlockSpec((1, d_diff), lambda b, j: (0, 0))],
        out_specs=[pl.BlockSpec((Ts, d_diff), lambda b, j: (b, 0)),
                   pl.BlockSpec((Ts * nmap, DIFF_DH), lambda b, j: (b, 0))],
        out_shape=[jax.ShapeDtypeStruct((nb * Ts, d_diff), BF16),
                   jax.ShapeDtypeStruct((nb * Ts * nmap, DIFF_DH), F32)],
        scratch_shapes=[pltpu.VMEM((Ts, d_diff), BF16),
                        pltpu.VMEM((nmap, nkb, Ts, kb), F32),
                        pltpu.VMEM((nmap, Ts, Ts), F32),
                        pltpu.VMEM((heads, nkb, Ts, kb), BF16),
                        pltpu.VMEM((Ts, d_diff), F32)],
        compiler_params=_params(("parallel", "arbitrary")),
        name="diff_sample",
    )(z, z, z, past_k, past_v, cos, sin, lp, gn)


def _mem_body(q_ref, k_ref, v_ref, o_ref, *, heads, dh):
    scale = dh ** -0.5
    for h in range(heads):
        hs = slice(h * dh, (h + 1) * dh)
        s = lax.dot_general(q_ref[:, hs], k_ref[:, hs].astype(BF16), (((1,), (1,)), ((), ())),
                            preferred_element_type=F32) * scale
        p = _softmax_rows(s).astype(BF16)
        o_ref[:, hs] = jnp.dot(p, v_ref[:, hs].astype(BF16), preferred_element_type=F32).astype(o_ref.dtype)


def _mem_attn(q, mem_k, mem_v, *, layer, nb, T, row0, heads):
    d = q.shape[1]
    n_mem = mem_k.shape[2]
    bt = min(T, 512)
    nt = T // bt
    rb0 = row0 // bt
    assert T % bt == 0 and row0 % bt == 0
    kv = pl.BlockSpec((None, None, n_mem, d), lambda b, t: (layer, b, 0, 0))
    return pl.pallas_call(
        functools.partial(_mem_body, heads=heads, dh=d // heads),
        grid=(nb, nt),
        in_specs=[pl.BlockSpec((bt, d), lambda b, t: (rb0 + b * nt + t, 0)), kv, kv],
        out_specs=pl.BlockSpec((bt, d), lambda b, t: (b * nt + t, 0)),
        out_shape=jax.ShapeDtypeStruct((nb * T, d), BF16),
        compiler_params=_params(("parallel", "parallel")),
        name="mem_attn",
    )(q, mem_k, mem_v)


def _rope_tables(pos):
    inv = ROPE_THETA ** (-jnp.arange(0, DIFF_DH, 2, dtype=F32) / DIFF_DH)
    ang = pos.astype(F32)[:, None] * inv[None, :]
    cos, sin = jnp.cos(ang), jnp.sin(ang)
    return jnp.concatenate([cos, cos], axis=1), jnp.concatenate([-sin, sin], axis=1)


def kernel(x_prompt, x_sample, cache_diff_k, cache_diff_v, state_gla, state_pool, cache_mem_k, cache_mem_v,
           mem_prompt, w_in, gla_w_a, gla_b_a, gla_norm, pool_w, pool_scale, diff_lambda, diff_norm, w_out,
           w_mq, w_mk, w_mv, w_mo, w_gate, w_up, w_down, ln_g, ln_b):
    Bp, Tp, D = x_prompt.shape
    Bs, Ts, _ = x_sample.shape
    depth = w_in.shape[0]
    past_len = cache_diff_k.shape[2]
    n_mem = mem_prompt.shape[1]
    mem_heads = cache_mem_k.shape[3]
    d_gla = gla_norm.shape[1]
    d_pool = pool_scale.shape[1]
    d_diff = diff_norm.shape[1]
    kw = gla_w_a.shape[2]
    lr = gla_w_a.shape[1]
    d_ff = w_gate.shape[2]
    alpha = (2 * depth) ** 0.25
    Mp, Ms = Bp * Tp, Bs * Ts
    assert lr <= LANE

    n_head = 2 * kw + 2 * d_gla
    n_main = n_head + 3 * d_diff + d_pool
    offs = {"q_g": 0, "k_g": kw, "v_g": 2 * kw, "r_g": 2 * kw + d_gla,
            "q_d": n_head, "k_d": n_head + d_diff, "v_d": n_head + 2 * d_diff, "u_p": n_head + 3 * d_diff,
            "a_lr": n_main}
    d_ff_pad = -(-d_ff // 1024) * 1024

    x = jnp.concatenate([x_prompt.reshape(Mp, D), x_sample.reshape(Ms, D)], axis=0)
    xb = x.astype(BF16)
    memb = mem_prompt.reshape(Bp * n_mem, D).astype(BF16)
    cos_p, sin_p = _rope_tables(jnp.arange(Tp))
    cos_s, sin_s = _rope_tables(past_len + jnp.arange(Ts))
    past_k = cache_diff_k.reshape(depth, Bs, past_len * (d_diff // DIFF_DH), DIFF_DH)
    past_v = jnp.transpose(cache_diff_v, (0, 1, 3, 2, 4))
    cmem_k = cache_mem_k.reshape(depth, Bs, n_mem, D)
    cmem_v = cache_mem_v.reshape(depth, Bs, n_mem, D)
    hist_s = jnp.pad(state_pool, ((0, 0), (0, 0), (1, 0), (0, 0)))

    outs = {k: [] for k in ("kp", "vp", "Sp", "hp", "mk", "mv", "ks", "vs", "Ss", "hs")}
    for l in range(depth):
        wi = w_in[l]
        w_in_r = jnp.concatenate([wi[:, :n_head], wi[:, n_head + lr:], wi[:, n_head:n_head + lr],
                                  jnp.zeros((D, LANE - lr), F32)], axis=1).astype(BF16)
        wa = jnp.pad(gla_w_a[l], ((0, LANE - lr), (0, 0))).astype(BF16)
        ba = gla_b_a[l].reshape(1, kw)
        gn_g = gla_norm[l].reshape(1, d_gla)
        gn_d = diff_norm[l].reshape(1, d_diff)
        pw = pool_w[l].astype(BF16)
        psc = pool_scale[l].reshape(1, d_pool)
        pad_ff = ((0, 0), (0, d_ff_pad - d_ff))
        wg = jnp.pad(w_gate[l].astype(BF16), pad_ff)
        wu = jnp.pad(w_up[l].astype(BF16), pad_ff)
        wd = jnp.pad(w_down[l].astype(BF16), ((0, d_ff_pad - d_ff), (0, 0)))

        z = _matmul(xb, w_in_r, name="mm_in", out_dtype=F32, bm=1024, bn=1152)
        yg_p, S_p = _gla(z, offs, wa, ba, gn_g, None, nb=Bp, T=Tp, L=CHUNK, row0=0, d_gla=d_gla)
        yg_s, S_s = _gla(z, offs, wa, ba, gn_g, state_gla[l], nb=Bs, T=Ts, L=Ts, row0=Mp, d_gla=d_gla)
        yp_p = _pool(z, offs["u_p"], None, pw, psc, nb=Bp, T=Tp, row0=0, pos0=0, d_pool=d_pool)
        yp_s = _pool(z, offs["u_p"], hist_s[l], pw, psc, nb=Bs, T=Ts, row0=Mp, pos0=past_len, d_pool=d_pool)
        yd_p, kr_p = _diff_prompt(z, offs, cos_p, sin_p, diff_lambda[l], gn_d, nb=Bp, T=Tp, layer=l, d_diff=d_diff)
        yd_s, kr_s = _diff_sample(z, offs, past_k, past_v, cos_s, sin_s, diff_lambda[l], gn_d,
                                  nb=Bs, Ts=Ts, row0=Mp, layer=l, d_diff=d_diff)
        y = _matmul_split([(yg_p, yg_s), (yp_p, yp_s), (yd_p, yd_s)], w_out[l].astype(BF16), x,
                          name="mm_out", alpha=alpha, bm=1024, bn=512)
        x, xb = _layer_norm(y, ln_g[l, 0], ln_b[l, 0])

        mk_p = _matmul(memb, w_mk[l].astype(BF16), name="mm_mk", out_dtype=F32, bm=1024, bn=1024)
        mv_p = _matmul(memb, w_mv[l].astype(BF16), name="mm_mv", out_dtype=F32, bm=1024, bn=1024)
        q = _matmul(xb, w_mq[l].astype(BF16), name="mm_mq", out_dtype=BF16, bm=1024, bn=1024)
        o_p = _mem_attn(q, mk_p.reshape(1, Bp, n_mem, D), mv_p.reshape(1, Bp, n_mem, D),
                        layer=0, nb=Bp, T=Tp, row0=0, heads=mem_heads)
        o_s = _mem_attn(q, cmem_k, cmem_v, layer=l, nb=Bs, T=Ts, row0=Mp, heads=mem_heads)
        y = _matmul_split([(o_p, o_s)], w_mo[l].astype(BF16), x, name="mm_mo", alpha=alpha, bm=1024, bn=512)
        x, xb = _layer_norm(y, ln_g[l, 1], ln_b[l, 1])

        hmid = _matmul(xb, wg, w2=wu, name="mm_gate_up", out_dtype=BF16, bm=1024, bn=512)
        y = _matmul(hmid, wd, name="mm_down", out_dtype=F32, bm=1024, bn=1024, bk=d_ff_pad // 4, res=x, alpha=alpha)
        x, xb = _layer_norm(y, ln_g[l, 2], ln_b[l, 2])

        hw = 2 * DIFF_DH
        nh = d_diff // hw
        v_all = z[:, offs["v_d"]:offs["v_d"] + d_diff]
        u_all = z[:, offs["u_p"]:offs["u_p"] + d_pool]
        outs["kp"].append(kr_p.reshape(Bp, Tp, nh, 2, DIFF_DH))
        outs["vp"].append(v_all[:Mp].reshape(Bp, Tp, nh, hw))
        outs["Sp"].append(S_p)
        outs["hp"].append(u_all[:Mp].reshape(Bp, Tp, d_pool)[:, Tp - POOL_HIST:])
        outs["mk"].append(mk_p.reshape(Bp, n_mem, mem_heads, D // mem_heads))
        outs["mv"].append(mv_p.reshape(Bp, n_mem, mem_heads, D // mem_heads))
        outs["ks"].append(kr_s.reshape(Bs, Ts, nh, 2, DIFF_DH))
        outs["vs"].append(v_all[Mp:].reshape(Bs, Ts, nh, hw))
        outs["Ss"].append(S_s)
        outs["hs"].append(u_all[Mp:].reshape(Bs, Ts, d_pool)[:, Ts - POOL_HIST:])

    st = {k: jnp.stack(v) for k, v in outs.items()}
    return (x[:Mp].reshape(Bp, Tp, D), x[Mp:].reshape(Bs, Ts, D),
            st["kp"], st["vp"], st["Sp"], st["hp"], st["mk"], st["mv"],
            st["ks"], st["vs"], st["Ss"], st["hs"])
```

```python
import functools
import math

import jax
import jax.numpy as jnp
from jax import lax
from jax.experimental import pallas as pl
from jax.experimental.pallas import tpu as pltpu

F32 = jnp.float32
BF16 = jnp.bfloat16

LANE = 128
V7X_VMEM_BYTES = 64 * 1024 * 1024
VMEM_LIMIT = V7X_VMEM_BYTES - 8 * 1024 * 1024

CHUNK = 64
GLA_HEADS = 4
GLA_TAU = 16.0
POOL_WINDOWS = (2, 4, 8, 16)
POOL_HIST = max(POOL_WINDOWS) - 1
DIFF_DH = 128
ROPE_THETA = 10000.0
NEG_INF = -1e30
LN_EPS = 1e-5
RMS_EPS = 1e-6


def _params(sem):
    return pltpu.CompilerParams(dimension_semantics=sem, vmem_limit_bytes=VMEM_LIMIT)


def _pick(dim, pref):
    if dim <= pref:
        return dim
    b = pref - pref % LANE
    while b >= LANE:
        if dim % b == 0:
            return b
        b -= LANE
    return dim


def _mm_body(*refs, nk, gated, has_res, alpha):
    it = iter(refs)
    x_ref = next(it)
    w_ref = next(it)
    w2_ref = next(it) if gated else None
    res_ref = next(it) if has_res else None
    o_ref = next(it)
    acc_ref = next(it) if nk > 1 else None
    acc2_ref = next(it) if (nk > 1 and gated) else None

    def finish(p, p2):
        if gated:
            p = p * (1.0 / (1.0 + jnp.exp(-p))) * p2
        if has_res:
            p = alpha * res_ref[...] + p
        o_ref[...] = p.astype(o_ref.dtype)

    x = x_ref[...]
    p = jnp.dot(x, w_ref[...], preferred_element_type=F32)
    p2 = jnp.dot(x, w2_ref[...], preferred_element_type=F32) if gated else None
    if nk == 1:
        finish(p, p2)
        return
    k = pl.program_id(2)

    @pl.when(k == 0)
    def _():
        acc_ref[...] = p
        if gated:
            acc2_ref[...] = p2

    @pl.when(k > 0)
    def _():
        acc_ref[...] += p
        if gated:
            acc2_ref[...] += p2

    @pl.when(k == nk - 1)
    def _():
        finish(acc_ref[...], acc2_ref[...] if gated else None)


def _matmul(x, w, *, layer, name, out_dtype, bm, bn, bk=None, w2=None, res=None, alpha=1.0):
    m, kdim = x.shape
    n = w.shape[2]
    bm = _pick(m, bm)
    bn = _pick(n, bn)
    bk = kdim if bk is None else _pick(kdim, bk)
    assert m % bm == 0 and n % bn == 0 and kdim % bk == 0 and w.shape[1] == kdim
    nk = kdim // bk
    gated = w2 is not None
    has_res = res is not None
    w_spec = pl.BlockSpec((None, bk, bn), lambda j, i, k: (layer, k, j))
    in_specs = [pl.BlockSpec((bm, bk), lambda j, i, k: (i, k)), w_spec]
    args = [x, w]
    if gated:
        in_specs.append(w_spec)
        args.append(w2)
    if has_res:
        in_specs.append(pl.BlockSpec((bm, bn), lambda j, i, k: (i, j)))
        args.append(res)
    scratch = []
    if nk > 1:
        scratch.append(pltpu.VMEM((bm, bn), F32))
        if gated:
            scratch.append(pltpu.VMEM((bm, bn), F32))
    return pl.pallas_call(
        functools.partial(_mm_body, nk=nk, gated=gated, has_res=has_res, alpha=alpha),
        grid=(n // bn, m // bm, nk),
        in_specs=in_specs,
        out_specs=pl.BlockSpec((bm, bn), lambda j, i, k: (i, j)),
        out_shape=jax.ShapeDtypeStruct((m, n), out_dtype),
        scratch_shapes=scratch,
        compiler_params=_params(("parallel", "parallel", "arbitrary")),
        name=name,
    )(*args)


def _mm_split_body(*refs, widths, nbp, alpha, res_split):
    npc = len(widths)
    p_refs, s_refs = refs[:npc], refs[npc:2 * npc]
    rest = refs[2 * npc:]
    w_ref = rest[0]
    res_p, res_s = (rest[1], rest[2]) if res_split else (rest[1], rest[1])
    o_ref = rest[-1]
    i = pl.program_id(1)

    def run(lhs_refs, res_ref):
        acc, k0 = None, 0
        for r, wd in zip(lhs_refs, widths):
            d = jnp.dot(r[...], w_ref[k0:k0 + wd, :], preferred_element_type=F32)
            acc = d if acc is None else acc + d
            k0 += wd
        o_ref[...] = (alpha * res_ref[...] + acc).astype(o_ref.dtype)

    @pl.when(i < nbp)
    def _():
        run(p_refs, res_p)

    @pl.when(i >= nbp)
    def _():
        run(s_refs, res_s)


def _matmul_split(pieces, w, res, *, layer, name, alpha, bm, bn):
    mp, ms = pieces[0][0].shape[0], pieces[0][1].shape[0]
    widths = tuple(p.shape[1] for p, _ in pieces)
    _, kdim, n = w.shape
    res_split = isinstance(res, tuple)
    assert sum(widths) == kdim
    bm = math.gcd(math.gcd(mp, ms), bm)
    bn = _pick(n, bn)
    nbp = mp // bm

    def p_map(j, i):
        return (jnp.minimum(i, nbp - 1), 0)

    def s_map(j, i):
        return (jnp.maximum(i - nbp, 0), 0)

    in_specs = [pl.BlockSpec((bm, wd), p_map) for wd in widths] + [pl.BlockSpec((bm, wd), s_map) for wd in widths]
    in_specs.append(pl.BlockSpec((None, kdim, bn), lambda j, i: (layer, 0, j)))
    if res_split:
        in_specs += [pl.BlockSpec((bm, bn), lambda j, i: (jnp.minimum(i, nbp - 1), j)),
                     pl.BlockSpec((bm, bn), lambda j, i: (jnp.maximum(i - nbp, 0), j))]
        res_args = list(res)
    else:
        in_specs.append(pl.BlockSpec((bm, bn), lambda j, i: (i, j)))
        res_args = [res]
    return pl.pallas_call(
        functools.partial(_mm_split_body, widths=widths, nbp=nbp, alpha=alpha, res_split=res_split),
        grid=(n // bn, (mp + ms) // bm),
        in_specs=in_specs,
        out_specs=pl.BlockSpec((bm, bn), lambda j, i: (i, j)),
        out_shape=jax.ShapeDtypeStruct((mp + ms, n), F32),
        compiler_params=_params(("parallel", "arbitrary")),
        name=name,
    )(*[p for p, _ in pieces], *[s for _, s in pieces], w, *res_args)


def _ln_body(y_ref, g_ref, b_ref, oa_ref, ob_ref, *, nbp):
    y = y_ref[...]
    mu = jnp.mean(y, axis=-1, keepdims=True)
    d = y - mu
    var = jnp.mean(d * d, axis=-1, keepdims=True)
    o = d * lax.rsqrt(var + LN_EPS) * g_ref[...] + b_ref[...]
    if nbp is None:
        oa_ref[...] = o
        ob_ref[...] = o.astype(BF16)
        return
    i = pl.program_id(0)

    @pl.when(i < nbp)
    def _():
        oa_ref[...] = o

    @pl.when(i >= nbp)
    def _():
        ob_ref[...] = o


def _layer_norm(y, g, b, *, split_rows=None, bm=256):
    m, d = y.shape
    vec = pl.BlockSpec((1, d), lambda i: (0, 0))
    if split_rows is None:
        bm = _pick(m, bm)
        nbp = None
        row = pl.BlockSpec((bm, d), lambda i: (i, 0))
        out_specs = [row, row]
        out_shape = [jax.ShapeDtypeStruct((m, d), F32), jax.ShapeDtypeStruct((m, d), BF16)]
    else:
        mp, ms = split_rows
        bm = math.gcd(math.gcd(mp, ms), bm)
        nbp = mp // bm
        row = pl.BlockSpec((bm, d), lambda i: (i, 0))
        out_specs = [pl.BlockSpec((bm, d), lambda i: (jnp.minimum(i, nbp - 1), 0)),
                     pl.BlockSpec((bm, d), lambda i: (jnp.maximum(i - nbp, 0), 0))]
        out_shape = [jax.ShapeDtypeStruct((mp, d), F32), jax.ShapeDtypeStruct((ms, d), F32)]
    return pl.pallas_call(
        functools.partial(_ln_body, nbp=nbp),
        grid=(m // bm,),
        in_specs=[row, vec, vec],
        out_specs=out_specs,
        out_shape=out_shape,
        compiler_params=_params(("arbitrary",)),
        name="layer_norm",
    )(y, g.reshape(1, d), b.reshape(1, d))


def _gla_body(*refs, heads, dk, dv, L, has_init):
    it = iter(refs)
    q_ref, k_ref, v_ref, r_ref, a_ref, wa_ref, ba_ref, gn_ref = (next(it) for _ in range(8))
    s0_ref = next(it) if has_init else None
    y_ref, sout_ref, s_scr = next(it), next(it), next(it)
    c = pl.program_id(1)

    @pl.when(c == 0)
    def _():
        if has_init:
            s_scr[...] = s0_ref[...]
        else:
            s_scr[...] = jnp.zeros_like(s_scr)

    glog = jnp.dot(a_ref[...].astype(BF16), wa_ref[...], preferred_element_type=F32) + ba_ref[...]
    lg = -(jnp.maximum(-glog, 0.0) + jnp.log1p(jnp.exp(-jnp.abs(glog)))) / GLA_TAU
    rows = lax.broadcasted_iota(jnp.int32, (L, L), 0)
    cols = lax.broadcasted_iota(jnp.int32, (L, L), 1)
    causal = rows >= cols
    bcum = jnp.dot(causal.astype(F32), lg, preferred_element_type=F32, precision=lax.Precision.HIGHEST)
    btot_col = lax.dot_general(lg, jnp.ones((L, LANE), F32), (((0,), (0,)), ((), ())),
                               preferred_element_type=F32, precision=lax.Precision.HIGHEST)
    btot_row = bcum[L - 1:L, :]
    qf = q_ref[...] * jnp.exp(bcum) * (dk ** -0.5)
    kf = k_ref[...]
    k_in = kf * jnp.exp(-bcum)
    k_dec = kf * jnp.exp(btot_row - bcum)
    v = v_ref[...]
    r = r_ref[...]
    for h in range(heads):
        ks = slice(h * dk, (h + 1) * dk)
        vs = slice(h * dv, (h + 1) * dv)
        qh = qf[:, ks].astype(BF16)
        vh = v[:, vs].astype(BF16)
        att = lax.dot_general(qh, k_in[:, ks].astype(BF16), (((1,), (1,)), ((), ())), preferred_element_type=F32)
        att = jnp.where(causal, att, 0.0)
        s_old = s_scr[h]
        o = (jnp.dot(qh, s_old.astype(BF16), preferred_element_type=F32)
             + jnp.dot(att.astype(BF16), vh, preferred_element_type=F32))
        o = o * lax.rsqrt(jnp.mean(o * o, axis=-1, keepdims=True) + RMS_EPS) * gn_ref[:, vs]
        rh = r[:, vs]
        y_ref[:, vs] = (o * (rh * (1.0 / (1.0 + jnp.exp(-rh))))).astype(y_ref.dtype)
        decay = jnp.exp(btot_col[ks, :])
        upd = lax.dot_general(k_dec[:, ks].astype(BF16), vh, (((0,), (0,)), ((), ())), preferred_element_type=F32)
        s_new = jnp.concatenate([s_old[:, j * LANE:(j + 1) * LANE] * decay for j in range(dv // LANE)], axis=1) + upd
        s_scr[h] = s_new
    sout_ref[...] = s_scr[...]


def _gla(z, a_lr, offs, wa, ba, gn, s0, *, nb, T, L, row0, d_gla):
    heads = GLA_HEADS
    dv = d_gla // heads
    dk = dv // 2
    kw = heads * dk
    nc = T // L
    rb0 = row0 // L
    has_init = s0 is not None
    assert row0 % L == 0 and dv % LANE == 0
    assert offs["q_g"] == 0 and offs["k_g"] == kw and offs["v_g"] == d_gla and offs["r_g"] == 2 * d_gla

    def rowblk(b, c):
        return rb0 + b * nc + c

    in_specs = [
        pl.BlockSpec((L, kw), lambda b, c: (rowblk(b, c), 0)),
        pl.BlockSpec((L, kw), lambda b, c: (rowblk(b, c), 1)),
        pl.BlockSpec((L, d_gla), lambda b, c: (rowblk(b, c), 1)),
        pl.BlockSpec((L, d_gla), lambda b, c: (rowblk(b, c), 2)),
        pl.BlockSpec((L, LANE), lambda b, c: (rowblk(b, c), 0)),
        pl.BlockSpec((LANE, kw), lambda b, c: (0, 0)),
        pl.BlockSpec((1, kw), lambda b, c: (0, 0)),
        pl.BlockSpec((1, d_gla), lambda b, c: (0, 0)),
    ]
    args = [z, z, z, z, a_lr, wa, ba, gn]
    if has_init:
        in_specs.append(pl.BlockSpec((None, heads, dk, dv), lambda b, c: (b, 0, 0, 0)))
        args.append(s0)
    return pl.pallas_call(
        functools.partial(_gla_body, heads=heads, dk=dk, dv=dv, L=L, has_init=has_init),
        grid=(nb, nc),
        in_specs=in_specs,
        out_specs=[pl.BlockSpec((L, d_gla), lambda b, c: (b * nc + c, 0)),
                   pl.BlockSpec((None, heads, dk, dv), lambda b, c: (b, 0, 0, 0))],
        out_shape=[jax.ShapeDtypeStruct((nb * T, d_gla), BF16),
                   jax.ShapeDtypeStruct((nb, heads, dk, dv), F32)],
        scratch_shapes=[pltpu.VMEM((heads, dk, dv), F32)],
        compiler_params=_params(("parallel", "arbitrary")),
        name="gla_init" if has_init else "gla",
    )(*args)


def _pool_body(*refs, Tt, nt, pos0, has_hist, gw):
    it = iter(refs)
    u_ref = next(it)
    h_ref = next(it) if has_hist else None
    w_ref, sc_ref, y_ref, hnew_ref, full = next(it), next(it), next(it), next(it), next(it)
    t = pl.program_id(1)
    H = POOL_HIST + 1

    @pl.when(t == 0)
    def _():
        if has_hist:
            full[0:H, :] = h_ref[...]
        else:
            full[0:H, :] = jnp.zeros((H, full.shape[1]), F32)

    @pl.when(t > 0)
    def _():
        full[0:H, :] = full[Tt:Tt + H, :]

    full[H:H + Tt, :] = u_ref[...]

    @pl.when(t == nt - 1)
    def _():
        hnew_ref[...] = full[Tt:Tt + H, :]

    pos = pos0 + t * Tt + lax.broadcasted_iota(jnp.int32, (Tt, 1), 0)
    for g, w in enumerate(POOL_WINDOWS):
        cs = slice(g * gw, (g + 1) * gw)
        cur = full[H:H + Tt, cs]
        acc = cur
        for j in range(1, w):
            acc = acc + full[H - j:H - j + Tt, cs]
        cnt = jnp.minimum(pos + 1, w).astype(F32)
        pooled = acc / cnt - cur
        y = jnp.dot(pooled.astype(BF16), w_ref[g], preferred_element_type=F32) * sc_ref[:, cs]
        y_ref[:, cs] = y.astype(y_ref.dtype)


def _pool(z, off_u, hist, w_pool, scale, *, nb, T, row0, pos0, d_pool):
    Tt = min(T, 512)
    assert T % Tt == 0 and row0 % Tt == 0 and off_u % d_pool == 0 and Tt >= POOL_HIST + 1
    nt = T // Tt
    rb0 = row0 // Tt
    ng = len(POOL_WINDOWS)
    gw = d_pool // ng
    has_hist = hist is not None
    in_specs = [pl.BlockSpec((Tt, d_pool), lambda b, t: (rb0 + b * nt + t, off_u // d_pool))]
    args = [z]
    if has_hist:
        in_specs.append(pl.BlockSpec((None, POOL_HIST + 1, d_pool), lambda b, t: (b, 0, 0)))
        args.append(hist)
    in_specs += [pl.BlockSpec((ng, gw, gw), lambda b, t: (0, 0, 0)),
                 pl.BlockSpec((1, d_pool), lambda b, t: (0, 0))]
    args += [w_pool, scale]
    return pl.pallas_call(
        functools.partial(_pool_body, Tt=Tt, nt=nt, pos0=pos0, has_hist=has_hist, gw=gw),
        grid=(nb, nt),
        in_specs=in_specs,
        out_specs=[pl.BlockSpec((Tt, d_pool), lambda b, t: (b * nt + t, 0)),
                   pl.BlockSpec((None, POOL_HIST + 1, d_pool), lambda b, t: (b, 0, 0))],
        out_shape=[jax.ShapeDtypeStruct((nb * T, d_pool), BF16),
                   jax.ShapeDtypeStruct((nb, POOL_HIST + 1, d_pool), F32)],
        scratch_shapes=[pltpu.VMEM((Tt + POOL_HIST + 1, d_pool), F32)],
        compiler_params=_params(("parallel", "arbitrary")),
        name="pool_hist" if has_hist else "pool",
    )(*args)


def _lam_init(layer):
    return 0.8 - 0.6 * math.exp(-0.3 * layer)


def _lam(lp_ref, layer):
    lp = lp_ref[...]
    a = jnp.sum(lp[0:1, :] * lp[1:2, :], axis=-1, keepdims=True)
    b = jnp.sum(lp[2:3, :] * lp[3:4, :], axis=-1, keepdims=True)
    return jnp.exp(a) - jnp.exp(b) + _lam_init(layer)


def _rope(x, cos, sin_signed):
    outs = []
    for m in range(2):
        xm = x[:, m * DIFF_DH:(m + 1) * DIFF_DH]
        outs.append(xm * cos + pltpu.roll(xm, DIFF_DH // 2, 1) * sin_signed)
    return jnp.concatenate(outs, axis=1)


def _softmax_rows(s):
    m = jnp.max(s, axis=-1, keepdims=True)
    e = jnp.exp(s - m)
    return e * (1.0 / jnp.sum(e, axis=-1, keepdims=True))


def _diff_finish(o, gn, layer):
    o = o * lax.rsqrt(jnp.mean(o * o, axis=-1, keepdims=True) + RMS_EPS) * gn
    return o * (1.0 - _lam_init(layer))


def _diffp_body(q_ref, k_ref, v_ref, cos_ref, sin_ref, lp_ref, gn_ref, y_ref, qb, kb, vb, *, T, bq, layer):
    dh = DIFF_DH
    lam = _lam(lp_ref, layer)
    cos = cos_ref[...]
    sin = sin_ref[...]
    kb[...] = _rope(k_ref[...], cos, sin).astype(BF16)
    qb[...] = _rope(q_ref[...], cos, sin).astype(BF16)
    vb[...] = v_ref[...].astype(BF16)
    scale = dh ** -0.5
    for i in range(T // bq):
        n = (i + 1) * bq
        qs = slice(i * bq, n)
        qc = (i * bq + lax.broadcasted_iota(jnp.int32, (bq, n), 0)) // CHUNK
        kc = lax.broadcasted_iota(jnp.int32, (bq, n), 1) // CHUNK
        mask = kc <= qc
        ps = []
        for m in range(2):
            ms = slice(m * dh, (m + 1) * dh)
            s = lax.dot_general(qb[qs, ms], kb[0:n, ms], (((1,), (1,)), ((), ())), preferred_element_type=F32)
            ps.append(_softmax_rows(jnp.where(mask, s * scale, NEG_INF)))
        p = (ps[0] - lam * ps[1]).astype(BF16)
        o = jnp.dot(p, vb[0:n, :], preferred_element_type=F32)
        y_ref[qs, :] = _diff_finish(o, gn_ref[...], layer).astype(y_ref.dtype)


def _diff_prompt(z, offs, cos, sin, lp, gn, *, nb, T, layer, d_diff):
    hw = 2 * DIFF_DH
    heads = d_diff // hw
    bq = min(T, 256)
    assert T % bq == 0 and bq % CHUNK == 0
    for name in ("q_d", "k_d", "v_d"):
        assert offs[name] % hw == 0

    def col(name):
        return lambda b, h: (b, offs[name] // hw + h)

    tab = pl.BlockSpec((T, DIFF_DH), lambda b, h: (0, 0))
    return pl.pallas_call(
        functools.partial(_diffp_body, T=T, bq=bq, layer=layer),
        grid=(nb, heads),
        in_specs=[pl.BlockSpec((T, hw), col("q_d")), pl.BlockSpec((T, hw), col("k_d")),
                  pl.BlockSpec((T, hw), col("v_d")), tab, tab,
                  pl.BlockSpec((4, DIFF_DH), lambda b, h: (0, 0)),
                  pl.BlockSpec((1, hw), lambda b, h: (0, h))],
        out_specs=pl.BlockSpec((T, hw), lambda b, h: (b, h)),
        out_shape=jax.ShapeDtypeStruct((nb * T, d_diff), BF16),
        scratch_shapes=[pltpu.VMEM((T, hw), BF16)] * 3,
        compiler_params=_params(("parallel", "parallel")),
        name="diff_prompt",
    )(z, z, z, cos, sin, lp, gn)


def _diffs_body(q_ref, k_ref, v_ref, pk_ref, pv_ref, cos_ref, sin_ref, lp_ref, gn_ref, y_ref,
                qb, s_past, s_new, p_past, acc, *, Ts, kb, nkb, heads, layer):
    dh = DIFF_DH
    hw = 2 * dh
    nmap = 2 * heads
    j = pl.program_id(1)
    scale = dh ** -0.5

    @pl.when(j == 0)
    def _():
        cos = cos_ref[...]
        sin = sin_ref[...]
        for c in range(nmap):
            cs = slice(c * dh, (c + 1) * dh)
            kc = k_ref[:, cs]
            kc = kc * cos + pltpu.roll(kc, dh // 2, 1) * sin
            qc = q_ref[:, cs]
            qc = (qc * cos + pltpu.roll(qc, dh // 2, 1) * sin).astype(BF16)
            qb[:, cs] = qc
            s_new[c] = lax.dot_general(qc, kc.astype(BF16), (((1,), (1,)), ((), ())),
                                       preferred_element_type=F32) * scale

    @pl.when(j < nkb)
    def _():
        for c in range(nmap):
            pk = pk_ref[pl.ds(c, kb, stride=nmap), :].astype(BF16)
            s_past[c, j] = lax.dot_general(qb[:, c * dh:(c + 1) * dh], pk, (((1,), (1,)), ((), ())),
                                           preferred_element_type=F32) * scale

    @pl.when(j == nkb)
    def _():
        lam = _lam(lp_ref, layer)
        for h in range(heads):
            es, en, inv = [], [], []
            for m in range(2):
                sp = s_past[2 * h + m]
                sn = s_new[2 * h + m]
                mx = jnp.maximum(jnp.max(jnp.max(sp, axis=0), axis=-1, keepdims=True),
                                 jnp.max(sn, axis=-1, keepdims=True))
                ep = jnp.exp(sp - mx[None])
                e_n = jnp.exp(sn - mx)
                tot = jnp.sum(jnp.sum(ep, axis=0), axis=-1, keepdims=True) + jnp.sum(e_n, axis=-1, keepdims=True)
                es.append(ep)
                en.append(e_n)
                inv.append(1.0 / tot)
            p_past[h] = (es[0] * inv[0][None] - lam * (es[1] * inv[1][None])).astype(BF16)
            pn = (en[0] * inv[0] - lam * (en[1] * inv[1])).astype(BF16)
            hs = slice(h * hw, (h + 1) * hw)
            acc[:, hs] = jnp.dot(pn, v_ref[:, hs].astype(BF16), preferred_element_type=F32)

    @pl.when(j >= nkb)
    def _():
        for h in range(heads):
            hs = slice(h * hw, (h + 1) * hw)
            acc[:, hs] += jnp.dot(p_past[h, j - nkb], pv_ref[h].astype(BF16), preferred_element_type=F32)

    @pl.when(j == 2 * nkb - 1)
    def _():
        for h in range(heads):
            hs = slice(h * hw, (h + 1) * hw)
            y_ref[:, hs] = _diff_finish(acc[:, hs], gn_ref[:, hs], layer).astype(y_ref.dtype)


def _diff_sample(z, offs, past_k, past_v, cos, sin, lp, gn, *, nb, Ts, row0, layer, d_diff):
    hw = 2 * DIFF_DH
    heads = d_diff // hw
    nmap = 2 * heads
    P = past_v.shape[3]
    kb = _pick(P, 1024)
    nkb = P // kb
    rb0 = row0 // Ts
    assert row0 % Ts == 0 and P % kb == 0
    for name in ("q_d", "k_d", "v_d"):
        assert offs[name] % d_diff == 0

    def col(name):
        return lambda b, j: (rb0 + b, offs[name] // d_diff)

    tab = pl.BlockSpec((Ts, DIFF_DH), lambda b, j: (0, 0))
    return pl.pallas_call(
        functools.partial(_diffs_body, Ts=Ts, kb=kb, nkb=nkb, heads=heads, layer=layer),
        grid=(nb, 2 * nkb),
        in_specs=[pl.BlockSpec((Ts, d_diff), col("q_d")), pl.BlockSpec((Ts, d_diff), col("k_d")),
                  pl.BlockSpec((Ts, d_diff), col("v_d")),
                  pl.BlockSpec((None, None, kb * nmap, DIFF_DH), lambda b, j: (layer, b, jnp.minimum(j, nkb - 1), 0)),
                  pl.BlockSpec((None, None, heads, kb, hw), lambda b, j: (layer, b, 0, jnp.maximum(j - nkb, 0), 0)),
                  tab, tab,
                  pl.BlockSpec((4, DIFF_DH), lambda b, j: (0, 0)),
                  pl.BlockSpec((1, d_diff), lambda b, j: (0, 0))],
        out_specs=pl.BlockSpec((Ts, d_diff), lambda b, j: (b, 0)),
        out_shape=jax.ShapeDtypeStruct((nb * Ts, d_diff), BF16),
        scratch_shapes=[pltpu.VMEM((Ts, d_diff), BF16),
                        pltpu.VMEM((nmap, nkb, Ts, kb), F32),
                        pltpu.VMEM((nmap, Ts, Ts), F32),
                        pltpu.VMEM((heads, nkb, Ts, kb), BF16),
                        pltpu.VMEM((Ts, d_diff), F32)],
        compiler_params=_params(("parallel", "arbitrary")),
        name="diff_sample",
    )(z, z, z, past_k, past_v, cos, sin, lp, gn)


def _kv_out_body(*refs, rt, heads, aliased):
    k_ref, v_ref, cos_ref, sin_ref = refs[:4]
    ko_ref, vo_ref = refs[-2:]
    dh = DIFF_DH
    nmap = 2 * heads
    cos = cos_ref[...]
    sin = sin_ref[...]
    for c in range(nmap):
        kc = k_ref[:, c * dh:(c + 1) * dh]
        ko_ref[pl.ds(c, rt, stride=nmap), :] = kc * cos + pltpu.roll(kc, dh // 2, 1) * sin
    for h in range(heads):
        vo_ref[h] = v_ref[:, h * 2 * dh:(h + 1) * 2 * dh]


def _kv_out(z, offs, cos, sin, prev, *, depth, layer, nb, T, row0, d_diff):
    hw = 2 * DIFF_DH
    heads = d_diff // hw
    nmap = 2 * heads
    rt = min(T, 512)
    nt = T // rt
    rb0 = row0 // rt
    assert T % rt == 0 and row0 % rt == 0 and offs["k_d"] % d_diff == 0 and offs["v_d"] % d_diff == 0
    aliased = prev is not None
    in_specs = [pl.BlockSpec((rt, d_diff), lambda b, t: (rb0 + b * nt + t, offs["k_d"] // d_diff)),
                pl.BlockSpec((rt, d_diff), lambda b, t: (rb0 + b * nt + t, offs["v_d"] // d_diff)),
                pl.BlockSpec((rt, DIFF_DH), lambda b, t: (t, 0)),
                pl.BlockSpec((rt, DIFF_DH), lambda b, t: (t, 0))]
    args = [z, z, cos, sin]
    aliases = {}
    if aliased:
        in_specs += [pl.BlockSpec(memory_space=pl.ANY)] * 2
        args += list(prev)
        aliases = {4: 0, 5: 1}
    return pl.pallas_call(
        functools.partial(_kv_out_body, rt=rt, heads=heads, aliased=aliased),
        grid=(nb, nt),
        in_specs=in_specs,
        out_specs=[pl.BlockSpec((None, rt * nmap, DIFF_DH), lambda b, t: (layer, b * nt + t, 0)),
                   pl.BlockSpec((None, None, heads, rt, hw), lambda b, t: (layer, b, 0, t, 0))],
        out_shape=[jax.ShapeDtypeStruct((depth, nb * T * nmap, DIFF_DH), F32),
                   jax.ShapeDtypeStruct((depth, nb, heads, T, hw), F32)],
        input_output_aliases=aliases,
        compiler_params=_params(("parallel", "parallel")),
        name="kv_out",
    )(*args)


def _mem_body(q_ref, k_ref, v_ref, o_ref, *, heads, dh):
    scale = dh ** -0.5
    for h in range(heads):
        hs = slice(h * dh, (h + 1) * dh)
        s = lax.dot_general(q_ref[:, hs], k_ref[:, hs].astype(BF16), (((1,), (1,)), ((), ())),
                            preferred_element_type=F32) * scale
        p = _softmax_rows(s).astype(BF16)
        o_ref[:, hs] = jnp.dot(p, v_ref[:, hs].astype(BF16), preferred_element_type=F32).astype(o_ref.dtype)


def _mem_attn(q, mem_k, mem_v, *, layer, nb, T, row0, heads):
    d = q.shape[1]
    n_mem = mem_k.shape[2]
    bt = min(T, 512)
    nt = T // bt
    rb0 = row0 // bt
    assert T % bt == 0 and row0 % bt == 0
    kv = pl.BlockSpec((None, None, n_mem, d), lambda b, t: (layer, b, 0, 0))
    return pl.pallas_call(
        functools.partial(_mem_body, heads=heads, dh=d // heads),
        grid=(nb, nt),
        in_specs=[pl.BlockSpec((bt, d), lambda b, t: (rb0 + b * nt + t, 0)), kv, kv],
        out_specs=pl.BlockSpec((bt, d), lambda b, t: (b * nt + t, 0)),
        out_shape=jax.ShapeDtypeStruct((nb * T, d), BF16),
        compiler_params=_params(("parallel", "parallel")),
        name="mem_attn",
    )(q, mem_k, mem_v)


def _rope_tables(pos):
    inv = ROPE_THETA ** (-jnp.arange(0, DIFF_DH, 2, dtype=F32) / DIFF_DH)
    ang = pos.astype(F32)[:, None] * inv[None, :]
    cos, sin = jnp.cos(ang), jnp.sin(ang)
    return jnp.concatenate([cos, cos], axis=1), jnp.concatenate([-sin, sin], axis=1)


def kernel(x_prompt, x_sample, cache_diff_k, cache_diff_v, state_gla, state_pool, cache_mem_k, cache_mem_v,
           mem_prompt, w_in, gla_w_a, gla_b_a, gla_norm, pool_w, pool_scale, diff_lambda, diff_norm, w_out,
           w_mq, w_mk, w_mv, w_mo, w_gate, w_up, w_down, ln_g, ln_b):
    Bp, Tp, D = x_prompt.shape
    Bs, Ts, _ = x_sample.shape
    depth = w_in.shape[0]
    past_len = cache_diff_k.shape[2]
    n_mem = mem_prompt.shape[1]
    mem_heads = cache_mem_k.shape[3]
    d_gla = gla_norm.shape[1]
    d_pool = pool_scale.shape[1]
    d_diff = diff_norm.shape[1]
    kw = gla_w_a.shape[2]
    lr = gla_w_a.shape[1]
    d_ff = w_gate.shape[2]
    alpha = (2 * depth) ** 0.25
    Mp, Ms = Bp * Tp, Bs * Ts
    assert lr <= LANE

    n_head = 2 * kw + 2 * d_gla
    offs = {"q_g": 0, "k_g": kw, "v_g": 2 * kw, "r_g": 2 * kw + d_gla,
            "q_d": n_head, "k_d": n_head + d_diff, "v_d": n_head + 2 * d_diff, "u_p": n_head + 3 * d_diff}
    d_ff_pad = -(-d_ff // 1024) * 1024
    hw = 2 * DIFF_DH
    nh = d_diff // hw

    w_main = jnp.concatenate([w_in[:, :, :n_head], w_in[:, :, n_head + lr:]], axis=2).astype(BF16)
    w_alr = jnp.pad(w_in[:, :, n_head:n_head + lr], ((0, 0), (0, 0), (0, LANE - lr))).astype(BF16)
    wa_all = jnp.pad(gla_w_a, ((0, 0), (0, LANE - lr), (0, 0))).astype(BF16)
    pw_all = pool_w.astype(BF16)
    wo_all, wq_all, wk_all, wv_all, wmo_all = (w.astype(BF16) for w in (w_out, w_mq, w_mk, w_mv, w_mo))
    wg_all = jnp.pad(w_gate.astype(BF16), ((0, 0), (0, 0), (0, d_ff_pad - d_ff)))
    wu_all = jnp.pad(w_up.astype(BF16), ((0, 0), (0, 0), (0, d_ff_pad - d_ff)))
    wd_all = jnp.pad(w_down.astype(BF16), ((0, 0), (0, d_ff_pad - d_ff), (0, 0)))

    x = (x_prompt.reshape(Mp, D), x_sample.reshape(Ms, D))
    xb = jnp.concatenate(x, axis=0).astype(BF16)
    memb = mem_prompt.reshape(Bp * n_mem, D).astype(BF16)
    cos_p, sin_p = _rope_tables(jnp.arange(Tp))
    cos_s, sin_s = _rope_tables(past_len + jnp.arange(Ts))
    past_k = cache_diff_k.reshape(depth, Bs, past_len * (d_diff // DIFF_DH), DIFF_DH)
    past_v = jnp.transpose(cache_diff_v, (0, 1, 3, 2, 4))
    cmem_k = cache_mem_k.reshape(depth, Bs, n_mem, D)
    cmem_v = cache_mem_v.reshape(depth, Bs, n_mem, D)
    hist_s = jnp.pad(state_pool, ((0, 0), (0, 0), (1, 0), (0, 0)))

    outs = {k: [] for k in ("Sp", "hp", "mk", "mv", "Ss", "hs")}
    kv_p = kv_s = None
    for l in range(depth):
        last = l == depth - 1
        ba = gla_b_a[l].reshape(1, kw)
        gn_g = gla_norm[l].reshape(1, d_gla)
        gn_d = diff_norm[l].reshape(1, d_diff)
        psc = pool_scale[l].reshape(1, d_pool)

        z = _matmul(xb, w_main, layer=l, name="mm_in", out_dtype=F32, bm=1024, bn=1024)
        a_lr = _matmul(xb, w_alr, layer=l, name="mm_alr", out_dtype=F32, bm=1024, bn=LANE)
        kv_p = _kv_out(z, offs, cos_p, sin_p, kv_p, depth=depth, layer=l, nb=Bp, T=Tp, row0=0, d_diff=d_diff)
        kv_s = _kv_out(z, offs, cos_s, sin_s, kv_s, depth=depth, layer=l, nb=Bs, T=Ts, row0=Mp, d_diff=d_diff)
        yg_p, S_p = _gla(z, a_lr, offs, wa_all[l], ba, gn_g, None, nb=Bp, T=Tp, L=CHUNK, row0=0, d_gla=d_gla)
        yg_s, S_s = _gla(z, a_lr, offs, wa_all[l], ba, gn_g, state_gla[l], nb=Bs, T=Ts, L=Ts, row0=Mp, d_gla=d_gla)
        yp_p, h_p = _pool(z, offs["u_p"], None, pw_all[l], psc, nb=Bp, T=Tp, row0=0, pos0=0, d_pool=d_pool)
        yp_s, h_s = _pool(z, offs["u_p"], hist_s[l], pw_all[l], psc, nb=Bs, T=Ts, row0=Mp, pos0=past_len,
                          d_pool=d_pool)
        yd_p = _diff_prompt(z, offs, cos_p, sin_p, diff_lambda[l], gn_d, nb=Bp, T=Tp, layer=l, d_diff=d_diff)
        yd_s = _diff_sample(z, offs, past_k, past_v, cos_s, sin_s, diff_lambda[l], gn_d,
                            nb=Bs, Ts=Ts, row0=Mp, layer=l, d_diff=d_diff)
        y = _matmul_split([(yg_p, yg_s), (yp_p, yp_s), (yd_p, yd_s)], wo_all, x, layer=l,
                          name="mm_out", alpha=alpha, bm=1024, bn=512)
        x, xb = _layer_norm(y, ln_g[l, 0], ln_b[l, 0])

        mk_p = _matmul(memb, wk_all, layer=l, name="mm_mk", out_dtype=F32, bm=1024, bn=1024)
        mv_p = _matmul(memb, wv_all, layer=l, name="mm_mv", out_dtype=F32, bm=1024, bn=1024)
        q = _matmul(xb, wq_all, layer=l, name="mm_mq", out_dtype=BF16, bm=1024, bn=1024)
        o_p = _mem_attn(q, mk_p.reshape(1, Bp, n_mem, D), mv_p.reshape(1, Bp, n_mem, D),
                        layer=0, nb=Bp, T=Tp, row0=0, heads=mem_heads)
        o_s = _mem_attn(q, cmem_k, cmem_v, layer=l, nb=Bs, T=Ts, row0=Mp, heads=mem_heads)
        y = _matmul_split([(o_p, o_s)], wmo_all, x, layer=l, name="mm_mo", alpha=alpha, bm=1024, bn=512)
        x, xb = _layer_norm(y, ln_g[l, 1], ln_b[l, 1])

        hmid = _matmul(xb, wg_all, w2=wu_all, layer=l, name="mm_gate_up", out_dtype=BF16, bm=1024, bn=512)
        y = _matmul(hmid, wd_all, layer=l, name="mm_down", out_dtype=F32, bm=1024, bn=1024, bk=d_ff_pad // 4,
                    res=x, alpha=alpha)
        if last:
            y_prompt, y_sample = _layer_norm(y, ln_g[l, 2], ln_b[l, 2], split_rows=(Mp, Ms))
        else:
            x, xb = _layer_norm(y, ln_g[l, 2], ln_b[l, 2])

        outs["Sp"].append(S_p)
        outs["hp"].append(h_p[:, 1:])
        outs["mk"].append(mk_p.reshape(Bp, n_mem, mem_heads, D // mem_heads))
        outs["mv"].append(mv_p.reshape(Bp, n_mem, mem_heads, D // mem_heads))
        outs["Ss"].append(S_s)
        outs["hs"].append(h_s[:, 1:])

    st = {k: jnp.stack(v) for k, v in outs.items()}
    new_k_p = kv_p[0].reshape(depth, Bp, Tp, nh, 2, DIFF_DH)
    new_v_p = jnp.transpose(kv_p[1], (0, 1, 3, 2, 4))
    new_k_s = kv_s[0].reshape(depth, Bs, Ts, nh, 2, DIFF_DH)
    new_v_s = jnp.transpose(kv_s[1], (0, 1, 3, 2, 4))
    return (y_prompt.reshape(Bp, Tp, D), y_sample.reshape(Bs, Ts, D),
            new_k_p, new_v_p, st["Sp"], st["hp"], st["mk"], st["mv"],
            new_k_s, new_v_s, st["Ss"], st["hs"])
```

```python
import functools
import math

import jax
import jax.numpy as jnp
from jax import lax
from jax.experimental import pallas as pl
from jax.experimental.pallas import tpu as pltpu

F32 = jnp.float32
BF16 = jnp.bfloat16

LANE = 128
V7X_VMEM_BYTES = 64 * 1024 * 1024
VMEM_LIMIT = V7X_VMEM_BYTES - 8 * 1024 * 1024

CHUNK = 64
GLA_HEADS = 4
GLA_TAU = 16.0
POOL_WINDOWS = (2, 4, 8, 16)
POOL_HIST = max(POOL_WINDOWS) - 1
DIFF_DH = 128
ROPE_THETA = 10000.0
NEG_INF = -1e30
LN_EPS = 1e-5
RMS_EPS = 1e-6


def _params(sem):
    return pltpu.CompilerParams(dimension_semantics=sem, vmem_limit_bytes=VMEM_LIMIT)


def _pick(dim, pref):
    if dim <= pref:
        return dim
    b = pref - pref % LANE
    while b >= LANE:
        if dim % b == 0:
            return b
        b -= LANE
    return dim


def _mm_body(*refs, nk, gated, has_res, alpha, cast_w):
    it = iter(refs)
    x_ref = next(it)
    w_ref = next(it)
    w2_ref = next(it) if gated else None
    res_ref = next(it) if has_res else None
    o_ref = next(it)
    acc_ref = next(it) if nk > 1 else None
    acc2_ref = next(it) if (nk > 1 and gated) else None
    wb_ref = next(it) if cast_w else None
    wb2_ref = next(it) if (cast_w and gated) else None

    def finish(p, p2):
        if gated:
            p = p * (1.0 / (1.0 + jnp.exp(-p))) * p2
        if has_res:
            p = alpha * res_ref[...] + p
        o_ref[...] = p.astype(o_ref.dtype)

    if cast_w:
        @pl.when(pl.program_id(1) == 0)
        def _():
            wb_ref[...] = w_ref[...].astype(BF16)
            if gated:
                wb2_ref[...] = w2_ref[...].astype(BF16)

        w_ref, w2_ref = wb_ref, wb2_ref

    x = x_ref[...]
    p = jnp.dot(x, w_ref[...], preferred_element_type=F32)
    p2 = jnp.dot(x, w2_ref[...], preferred_element_type=F32) if gated else None
    if nk == 1:
        finish(p, p2)
        return
    k = pl.program_id(2)

    @pl.when(k == 0)
    def _():
        acc_ref[...] = p
        if gated:
            acc2_ref[...] = p2

    @pl.when(k > 0)
    def _():
        acc_ref[...] += p
        if gated:
            acc2_ref[...] += p2

    @pl.when(k == nk - 1)
    def _():
        finish(acc_ref[...], acc2_ref[...] if gated else None)


def _matmul(x, w, *, layer, name, out_dtype, bm, bn, bk=None, w2=None, res=None, alpha=1.0):
    m, kdim = x.shape
    n = w.shape[2]
    bm = _pick(m, bm)
    bn = _pick(n, bn)
    bk = kdim if bk is None else _pick(kdim, bk)
    assert m % bm == 0 and n % bn == 0 and kdim % bk == 0 and w.shape[1] == kdim
    nk = kdim // bk
    gated = w2 is not None
    has_res = res is not None
    cast_w = w.dtype == F32
    assert not (cast_w and nk > 1)
    w_spec = pl.BlockSpec((None, bk, bn), lambda j, i, k: (layer, k, j))
    in_specs = [pl.BlockSpec((bm, bk), lambda j, i, k: (i, k)), w_spec]
    args = [x, w]
    if gated:
        in_specs.append(w_spec)
        args.append(w2)
    if has_res:
        in_specs.append(pl.BlockSpec((bm, bn), lambda j, i, k: (i, j)))
        args.append(res)
    scratch = []
    if nk > 1:
        scratch += [pltpu.VMEM((bm, bn), F32)] * (2 if gated else 1)
    if cast_w:
        scratch += [pltpu.VMEM((bk, bn), BF16)] * (2 if gated else 1)
    return pl.pallas_call(
        functools.partial(_mm_body, nk=nk, gated=gated, has_res=has_res, alpha=alpha, cast_w=cast_w),
        grid=(n // bn, m // bm, nk),
        in_specs=in_specs,
        out_specs=pl.BlockSpec((bm, bn), lambda j, i, k: (i, j)),
        out_shape=jax.ShapeDtypeStruct((m, n), out_dtype),
        scratch_shapes=scratch,
        compiler_params=_params(("parallel", "arbitrary" if cast_w else "parallel", "arbitrary")),
        name=name,
    )(*args)


def _mm_split_body(*refs, widths, nbp, alpha, res_split):
    npc = len(widths)
    p_refs, s_refs = refs[:npc], refs[npc:2 * npc]
    rest = refs[2 * npc:]
    w_ref = rest[0]
    res_p, res_s = (rest[1], rest[2]) if res_split else (rest[1], rest[1])
    o_ref = rest[-1]
    i = pl.program_id(1)

    def run(lhs_refs, res_ref):
        acc, k0 = None, 0
        for r, wd in zip(lhs_refs, widths):
            d = jnp.dot(r[...], w_ref[k0:k0 + wd, :], preferred_element_type=F32)
            acc = d if acc is None else acc + d
            k0 += wd
        o_ref[...] = (alpha * res_ref[...] + acc).astype(o_ref.dtype)

    @pl.when(i < nbp)
    def _():
        run(p_refs, res_p)

    @pl.when(i >= nbp)
    def _():
        run(s_refs, res_s)


def _matmul_split(pieces, w, res, *, layer, name, alpha, bm, bn):
    mp, ms = pieces[0][0].shape[0], pieces[0][1].shape[0]
    widths = tuple(p.shape[1] for p, _ in pieces)
    _, kdim, n = w.shape
    res_split = isinstance(res, tuple)
    assert sum(widths) == kdim
    bm = math.gcd(math.gcd(mp, ms), bm)
    bn = _pick(n, bn)
    nbp = mp // bm

    def p_map(j, i):
        return (jnp.minimum(i, nbp - 1), 0)

    def s_map(j, i):
        return (jnp.maximum(i - nbp, 0), 0)

    in_specs = [pl.BlockSpec((bm, wd), p_map) for wd in widths] + [pl.BlockSpec((bm, wd), s_map) for wd in widths]
    in_specs.append(pl.BlockSpec((None, kdim, bn), lambda j, i: (layer, 0, j)))
    if res_split:
        in_specs += [pl.BlockSpec((bm, bn), lambda j, i: (jnp.minimum(i, nbp - 1), j)),
                     pl.BlockSpec((bm, bn), lambda j, i: (jnp.maximum(i - nbp, 0), j))]
        res_args = list(res)
    else:
        in_specs.append(pl.BlockSpec((bm, bn), lambda j, i: (i, j)))
        res_args = [res]
    return pl.pallas_call(
        functools.partial(_mm_split_body, widths=widths, nbp=nbp, alpha=alpha, res_split=res_split),
        grid=(n // bn, (mp + ms) // bm),
        in_specs=in_specs,
        out_specs=pl.BlockSpec((bm, bn), lambda j, i: (i, j)),
        out_shape=jax.ShapeDtypeStruct((mp + ms, n), F32),
        compiler_params=_params(("parallel", "arbitrary")),
        name=name,
    )(*[p for p, _ in pieces], *[s for _, s in pieces], w, *res_args)


def _ln_body(y_ref, g_ref, b_ref, oa_ref, ob_ref, *, nbp):
    y = y_ref[...]
    mu = jnp.mean(y, axis=-1, keepdims=True)
    d = y - mu
    var = jnp.mean(d * d, axis=-1, keepdims=True)
    o = d * lax.rsqrt(var + LN_EPS) * g_ref[...] + b_ref[...]
    if nbp is None:
        oa_ref[...] = o
        ob_ref[...] = o.astype(BF16)
        return
    i = pl.program_id(0)

    @pl.when(i < nbp)
    def _():
        oa_ref[...] = o

    @pl.when(i >= nbp)
    def _():
        ob_ref[...] = o


def _layer_norm(y, g, b, *, split_rows=None, bm=256):
    m, d = y.shape
    vec = pl.BlockSpec((1, d), lambda i: (0, 0))
    if split_rows is None:
        bm = _pick(m, bm)
        nbp = None
        row = pl.BlockSpec((bm, d), lambda i: (i, 0))
        out_specs = [row, row]
        out_shape = [jax.ShapeDtypeStruct((m, d), F32), jax.ShapeDtypeStruct((m, d), BF16)]
    else:
        mp, ms = split_rows
        bm = math.gcd(math.gcd(mp, ms), bm)
        nbp = mp // bm
        row = pl.BlockSpec((bm, d), lambda i: (i, 0))
        out_specs = [pl.BlockSpec((bm, d), lambda i: (jnp.minimum(i, nbp - 1), 0)),
                     pl.BlockSpec((bm, d), lambda i: (jnp.maximum(i - nbp, 0), 0))]
        out_shape = [jax.ShapeDtypeStruct((mp, d), F32), jax.ShapeDtypeStruct((ms, d), F32)]
    return pl.pallas_call(
        functools.partial(_ln_body, nbp=nbp),
        grid=(m // bm,),
        in_specs=[row, vec, vec],
        out_specs=out_specs,
        out_shape=out_shape,
        compiler_params=_params(("arbitrary",)),
        name="layer_norm",
    )(y, g.reshape(1, d), b.reshape(1, d))


def _gla_body(*refs, heads, dk, dv, L, has_init, aliased):
    it = iter(refs)
    q_ref, k_ref, v_ref, r_ref, a_ref, wa_ref, ba_ref, gn_ref = (next(it) for _ in range(8))
    s0_ref = next(it) if has_init else None
    if aliased:
        next(it)
    y_ref, sout_ref, s_scr = next(it), next(it), next(it)
    c = pl.program_id(1)

    @pl.when(c == 0)
    def _():
        if has_init:
            s_scr[...] = s0_ref[...]
        else:
            s_scr[...] = jnp.zeros_like(s_scr)

    glog = jnp.dot(a_ref[...].astype(BF16), wa_ref[...], preferred_element_type=F32) + ba_ref[...]
    lg = -(jnp.maximum(-glog, 0.0) + jnp.log1p(jnp.exp(-jnp.abs(glog)))) / GLA_TAU
    rows = lax.broadcasted_iota(jnp.int32, (L, L), 0)
    cols = lax.broadcasted_iota(jnp.int32, (L, L), 1)
    causal = rows >= cols
    bcum = jnp.dot(causal.astype(F32), lg, preferred_element_type=F32, precision=lax.Precision.HIGHEST)
    btot_col = lax.dot_general(lg, jnp.ones((L, LANE), F32), (((0,), (0,)), ((), ())),
                               preferred_element_type=F32, precision=lax.Precision.HIGHEST)
    btot_row = bcum[L - 1:L, :]
    qf = q_ref[...] * jnp.exp(bcum) * (dk ** -0.5)
    kf = k_ref[...]
    k_in = kf * jnp.exp(-bcum)
    k_dec = kf * jnp.exp(btot_row - bcum)
    v = v_ref[...]
    r = r_ref[...]
    for h in range(heads):
        ks = slice(h * dk, (h + 1) * dk)
        vs = slice(h * dv, (h + 1) * dv)
        qh = qf[:, ks].astype(BF16)
        vh = v[:, vs].astype(BF16)
        att = lax.dot_general(qh, k_in[:, ks].astype(BF16), (((1,), (1,)), ((), ())), preferred_element_type=F32)
        att = jnp.where(causal, att, 0.0)
        s_old = s_scr[h]
        o = (jnp.dot(qh, s_old.astype(BF16), preferred_element_type=F32)
             + jnp.dot(att.astype(BF16), vh, preferred_element_type=F32))
        o = o * lax.rsqrt(jnp.mean(o * o, axis=-1, keepdims=True) + RMS_EPS) * gn_ref[:, vs]
        rh = r[:, vs]
        y_ref[:, vs] = (o * (rh * (1.0 / (1.0 + jnp.exp(-rh))))).astype(y_ref.dtype)
        decay = jnp.exp(btot_col[ks, :])
        upd = lax.dot_general(k_dec[:, ks].astype(BF16), vh, (((0,), (0,)), ((), ())), preferred_element_type=F32)
        s_new = jnp.concatenate([s_old[:, j * LANE:(j + 1) * LANE] * decay for j in range(dv // LANE)], axis=1) + upd
        s_scr[h] = s_new
    sout_ref[...] = s_scr[...]


def _gla(z, a_lr, offs, wa, ba, gn, s0, s_prev, *, depth, layer, nb, T, L, row0, d_gla):
    heads = GLA_HEADS
    dv = d_gla // heads
    dk = dv // 2
    kw = heads * dk
    nc = T // L
    rb0 = row0 // L
    has_init = s0 is not None
    assert row0 % L == 0 and dv % LANE == 0
    assert offs["q_g"] == 0 and offs["k_g"] == kw and offs["v_g"] == d_gla and offs["r_g"] == 2 * d_gla

    def rowblk(b, c):
        return rb0 + b * nc + c

    in_specs = [
        pl.BlockSpec((L, kw), lambda b, c: (rowblk(b, c), 0)),
        pl.BlockSpec((L, kw), lambda b, c: (rowblk(b, c), 1)),
        pl.BlockSpec((L, d_gla), lambda b, c: (rowblk(b, c), 1)),
        pl.BlockSpec((L, d_gla), lambda b, c: (rowblk(b, c), 2)),
        pl.BlockSpec((L, LANE), lambda b, c: (rowblk(b, c), 0)),
        pl.BlockSpec((LANE, kw), lambda b, c: (0, 0)),
        pl.BlockSpec((1, kw), lambda b, c: (0, 0)),
        pl.BlockSpec((1, d_gla), lambda b, c: (0, 0)),
    ]
    args = [z, z, z, z, a_lr, wa, ba, gn]
    state_spec = pl.BlockSpec((None, None, heads, dk, dv), lambda b, c: (layer, b, 0, 0, 0))
    if has_init:
        in_specs.append(state_spec)
        args.append(s0)
    aliases = {}
    if s_prev is not None:
        aliases = {len(args): 1}
        in_specs.append(pl.BlockSpec(memory_space=pl.ANY))
        args.append(s_prev)
    return pl.pallas_call(
        functools.partial(_gla_body, heads=heads, dk=dk, dv=dv, L=L, has_init=has_init,
                          aliased=s_prev is not None),
        grid=(nb, nc),
        in_specs=in_specs,
        out_specs=[pl.BlockSpec((L, d_gla), lambda b, c: (b * nc + c, 0)), state_spec],
        out_shape=[jax.ShapeDtypeStruct((nb * T, d_gla), BF16),
                   jax.ShapeDtypeStruct((depth, nb, heads, dk, dv), F32)],
        scratch_shapes=[pltpu.VMEM((heads, dk, dv), F32)],
        input_output_aliases=aliases,
        compiler_params=_params(("parallel", "arbitrary")),
        name="gla_init" if has_init else "gla",
    )(*args)


def _pool_body(*refs, Tt, nt, pos0, has_hist, gw):
    it = iter(refs)
    u_ref = next(it)
    h_ref = next(it) if has_hist else None
    w_ref, sc_ref, y_ref, hnew_ref, full = next(it), next(it), next(it), next(it), next(it)
    t = pl.program_id(1)
    H = POOL_HIST + 1

    @pl.when(t == 0)
    def _():
        if has_hist:
            full[0:H, :] = h_ref[...]
        else:
            full[0:H, :] = jnp.zeros((H, full.shape[1]), F32)

    @pl.when(t > 0)
    def _():
        full[0:H, :] = full[Tt:Tt + H, :]

    full[H:H + Tt, :] = u_ref[...]

    @pl.when(t == nt - 1)
    def _():
        hnew_ref[...] = full[Tt:Tt + H, :]

    pos = pos0 + t * Tt + lax.broadcasted_iota(jnp.int32, (Tt, 1), 0)
    for g, w in enumerate(POOL_WINDOWS):
        cs = slice(g * gw, (g + 1) * gw)
        cur = full[H:H + Tt, cs]
        acc = cur
        for j in range(1, w):
            acc = acc + full[H - j:H - j + Tt, cs]
        cnt = jnp.minimum(pos + 1, w).astype(F32)
        pooled = acc / cnt - cur
        y = jnp.dot(pooled.astype(BF16), w_ref[g], preferred_element_type=F32) * sc_ref[:, cs]
        y_ref[:, cs] = y.astype(y_ref.dtype)


def _pool(z, off_u, hist, w_pool, scale, *, nb, T, row0, pos0, d_pool):
    Tt = min(T, 512)
    assert T % Tt == 0 and row0 % Tt == 0 and off_u % d_pool == 0 and Tt >= POOL_HIST + 1
    nt = T // Tt
    rb0 = row0 // Tt
    ng = len(POOL_WINDOWS)
    gw = d_pool // ng
    has_hist = hist is not None
    in_specs = [pl.BlockSpec((Tt, d_pool), lambda b, t: (rb0 + b * nt + t, off_u // d_pool))]
    args = [z]
    if has_hist:
        in_specs.append(pl.BlockSpec((None, POOL_HIST + 1, d_pool), lambda b, t: (b, 0, 0)))
        args.append(hist)
    in_specs += [pl.BlockSpec((ng, gw, gw), lambda b, t: (0, 0, 0)),
                 pl.BlockSpec((1, d_pool), lambda b, t: (0, 0))]
    args += [w_pool, scale]
    return pl.pallas_call(
        functools.partial(_pool_body, Tt=Tt, nt=nt, pos0=pos0, has_hist=has_hist, gw=gw),
        grid=(nb, nt),
        in_specs=in_specs,
        out_specs=[pl.BlockSpec((Tt, d_pool), lambda b, t: (b * nt + t, 0)),
                   pl.BlockSpec((None, POOL_HIST + 1, d_pool), lambda b, t: (b, 0, 0))],
        out_shape=[jax.ShapeDtypeStruct((nb * T, d_pool), BF16),
                   jax.ShapeDtypeStruct((nb, POOL_HIST + 1, d_pool), F32)],
        scratch_shapes=[pltpu.VMEM((Tt + POOL_HIST + 1, d_pool), F32)],
        compiler_params=_params(("parallel", "arbitrary")),
        name="pool_hist" if has_hist else "pool",
    )(*args)


def _lam_init(layer):
    return 0.8 - 0.6 * math.exp(-0.3 * layer)


def _lam(lp_ref, layer):
    lp = lp_ref[...]
    a = jnp.sum(lp[0:1, :] * lp[1:2, :], axis=-1, keepdims=True)
    b = jnp.sum(lp[2:3, :] * lp[3:4, :], axis=-1, keepdims=True)
    return jnp.exp(a) - jnp.exp(b) + _lam_init(layer)


def _rope(x, cos, sin_signed):
    outs = []
    for m in range(2):
        xm = x[:, m * DIFF_DH:(m + 1) * DIFF_DH]
        outs.append(xm * cos + pltpu.roll(xm, DIFF_DH // 2, 1) * sin_signed)
    return jnp.concatenate(outs, axis=1)


def _softmax_rows(s):
    m = jnp.max(s, axis=-1, keepdims=True)
    e = jnp.exp(s - m)
    return e * (1.0 / jnp.sum(e, axis=-1, keepdims=True))


def _diff_finish(o, gn, layer):
    o = o * lax.rsqrt(jnp.mean(o * o, axis=-1, keepdims=True) + RMS_EPS) * gn
    return o * (1.0 - _lam_init(layer))


def _diffp_body(q_ref, k_ref, v_ref, cos_ref, sin_ref, lp_ref, gn_ref, y_ref, qb, kb, vb, *, T, bq, layer):
    dh = DIFF_DH
    lam = _lam(lp_ref, layer)
    cos = cos_ref[...]
    sin = sin_ref[...]
    kb[...] = _rope(k_ref[...], cos, sin).astype(BF16)
    qb[...] = _rope(q_ref[...], cos, sin).astype(BF16)
    vb[...] = v_ref[...].astype(BF16)
    scale = dh ** -0.5
    for i in range(T // bq):
        n = (i + 1) * bq
        qs = slice(i * bq, n)
        qc = (i * bq + lax.broadcasted_iota(jnp.int32, (bq, n), 0)) // CHUNK
        kc = lax.broadcasted_iota(jnp.int32, (bq, n), 1) // CHUNK
        mask = kc <= qc
        ps = []
        for m in range(2):
            ms = slice(m * dh, (m + 1) * dh)
            s = lax.dot_general(qb[qs, ms], kb[0:n, ms], (((1,), (1,)), ((), ())), preferred_element_type=F32)
            ps.append(_softmax_rows(jnp.where(mask, s * scale, NEG_INF)))
        p = (ps[0] - lam * ps[1]).astype(BF16)
        o = jnp.dot(p, vb[0:n, :], preferred_element_type=F32)
        y_ref[qs, :] = _diff_finish(o, gn_ref[...], layer).astype(y_ref.dtype)


def _diff_prompt(z, offs, cos, sin, lp, gn, *, nb, T, layer, d_diff):
    hw = 2 * DIFF_DH
    heads = d_diff // hw
    bq = min(T, 256)
    assert T % bq == 0 and bq % CHUNK == 0
    for name in ("q_d", "k_d", "v_d"):
        assert offs[name] % hw == 0

    def col(name):
        return lambda b, h: (b, offs[name] // hw + h)

    tab = pl.BlockSpec((T, DIFF_DH), lambda b, h: (0, 0))
    return pl.pallas_call(
        functools.partial(_diffp_body, T=T, bq=bq, layer=layer),
        grid=(nb, heads),
        in_specs=[pl.BlockSpec((T, hw), col("q_d")), pl.BlockSpec((T, hw), col("k_d")),
                  pl.BlockSpec((T, hw), col("v_d")), tab, tab,
                  pl.BlockSpec((4, DIFF_DH), lambda b, h: (0, 0)),
                  pl.BlockSpec((1, hw), lambda b, h: (0, h))],
        out_specs=pl.BlockSpec((T, hw), lambda b, h: (b, h)),
        out_shape=jax.ShapeDtypeStruct((nb * T, d_diff), BF16),
        scratch_shapes=[pltpu.VMEM((T, hw), BF16)] * 3,
        compiler_params=_params(("parallel", "parallel")),
        name="diff_prompt",
    )(z, z, z, cos, sin, lp, gn)


def _diffs_body(q_ref, k_ref, v_ref, pk_ref, pv_ref, cos_ref, sin_ref, lp_ref, gn_ref, y_ref,
                qb, s_past, s_new, p_past, acc, *, Ts, kb, nkb, heads, layer):
    dh = DIFF_DH
    hw = 2 * dh
    nmap = 2 * heads
    j = pl.program_id(1)
    scale = dh ** -0.5

    @pl.when(j == 0)
    def _():
        cos = cos_ref[...]
        sin = sin_ref[...]
        for c in range(nmap):
            cs = slice(c * dh, (c + 1) * dh)
            kc = k_ref[:, cs]
            kc = kc * cos + pltpu.roll(kc, dh // 2, 1) * sin
            qc = q_ref[:, cs]
            qc = (qc * cos + pltpu.roll(qc, dh // 2, 1) * sin).astype(BF16)
            qb[:, cs] = qc
            s_new[c] = lax.dot_general(qc, kc.astype(BF16), (((1,), (1,)), ((), ())),
                                       preferred_element_type=F32) * scale

    @pl.when(j < nkb)
    def _():
        for c in range(nmap):
            pk = pk_ref[pl.ds(c, kb, stride=nmap), :].astype(BF16)
            s_past[c, j] = lax.dot_general(qb[:, c * dh:(c + 1) * dh], pk, (((1,), (1,)), ((), ())),
                                           preferred_element_type=F32) * scale

    @pl.when(j == nkb)
    def _():
        lam = _lam(lp_ref, layer)
        for h in range(heads):
            es, en, inv = [], [], []
            for m in range(2):
                sp = s_past[2 * h + m]
                sn = s_new[2 * h + m]
                mx = jnp.maximum(jnp.max(jnp.max(sp, axis=0), axis=-1, keepdims=True),
                                 jnp.max(sn, axis=-1, keepdims=True))
                ep = jnp.exp(sp - mx[None])
                e_n = jnp.exp(sn - mx)
                tot = jnp.sum(jnp.sum(ep, axis=0), axis=-1, keepdims=True) + jnp.sum(e_n, axis=-1, keepdims=True)
                es.append(ep)
                en.append(e_n)
                inv.append(1.0 / tot)
            p_past[h] = (es[0] * inv[0][None] - lam * (es[1] * inv[1][None])).astype(BF16)
            pn = (en[0] * inv[0] - lam * (en[1] * inv[1])).astype(BF16)
            hs = slice(h * hw, (h + 1) * hw)
            acc[:, hs] = jnp.dot(pn, v_ref[:, hs].astype(BF16), preferred_element_type=F32)

    @pl.when(j >= nkb)
    def _():
        for h in range(heads):
            hs = slice(h * hw, (h + 1) * hw)
            acc[:, hs] += jnp.dot(p_past[h, j - nkb], pv_ref[h].astype(BF16), preferred_element_type=F32)

    @pl.when(j == 2 * nkb - 1)
    def _():
        for h in range(heads):
            hs = slice(h * hw, (h + 1) * hw)
            y_ref[:, hs] = _diff_finish(acc[:, hs], gn_ref[:, hs], layer).astype(y_ref.dtype)


def _diff_sample(z, offs, past_k, past_v, cos, sin, lp, gn, *, nb, Ts, row0, layer, d_diff):
    hw = 2 * DIFF_DH
    heads = d_diff // hw
    nmap = 2 * heads
    P = past_v.shape[3]
    kb = _pick(P, 1024)
    nkb = P // kb
    rb0 = row0 // Ts
    assert row0 % Ts == 0 and P % kb == 0
    for name in ("q_d", "k_d", "v_d"):
        assert offs[name] % d_diff == 0

    def col(name):
        return lambda b, j: (rb0 + b, offs[name] // d_diff)

    tab = pl.BlockSpec((Ts, DIFF_DH), lambda b, j: (0, 0))
    return pl.pallas_call(
        functools.partial(_diffs_body, Ts=Ts, kb=kb, nkb=nkb, heads=heads, layer=layer),
        grid=(nb, 2 * nkb),
        in_specs=[pl.BlockSpec((Ts, d_diff), col("q_d")), pl.BlockSpec((Ts, d_diff), col("k_d")),
                  pl.BlockSpec((Ts, d_diff), col("v_d")),
                  pl.BlockSpec((None, None, kb * nmap, DIFF_DH), lambda b, j: (layer, b, jnp.minimum(j, nkb - 1), 0)),
                  pl.BlockSpec((None, None, heads, kb, hw), lambda b, j: (layer, b, 0, jnp.maximum(j - nkb, 0), 0)),
                  tab, tab,
                  pl.BlockSpec((4, DIFF_DH), lambda b, j: (0, 0)),
                  pl.BlockSpec((1, d_diff), lambda b, j: (0, 0))],
        out_specs=pl.BlockSpec((Ts, d_diff), lambda b, j: (b, 0)),
        out_shape=jax.ShapeDtypeStruct((nb * Ts, d_diff), BF16),
        scratch_shapes=[pltpu.VMEM((Ts, d_diff), BF16),
                        pltpu.VMEM((nmap, nkb, Ts, kb), F32),
                        pltpu.VMEM((nmap, Ts, Ts), F32),
                        pltpu.VMEM((heads, nkb, Ts, kb), BF16),
                        pltpu.VMEM((Ts, d_diff), F32)],
        compiler_params=_params(("parallel", "arbitrary")),
        name="diff_sample",
    )(z, z, z, past_k, past_v, cos, sin, lp, gn)


def _kv_out_body(*refs, rt, heads, aliased):
    k_ref, v_ref, cos_ref, sin_ref = refs[:4]
    ko_ref, vo_ref = refs[-2:]
    dh = DIFF_DH
    nmap = 2 * heads
    cos = cos_ref[...]
    sin = sin_ref[...]
    for c in range(nmap):
        kc = k_ref[:, c * dh:(c + 1) * dh]
        ko_ref[pl.ds(c, rt, stride=nmap), :] = kc * cos + pltpu.roll(kc, dh // 2, 1) * sin
    for h in range(heads):
        vo_ref[h] = v_ref[:, h * 2 * dh:(h + 1) * 2 * dh]


def _kv_out(z, offs, cos, sin, prev, *, depth, layer, nb, T, row0, d_diff):
    hw = 2 * DIFF_DH
    heads = d_diff // hw
    nmap = 2 * heads
    rt = min(T, 512)
    nt = T // rt
    rb0 = row0 // rt
    assert T % rt == 0 and row0 % rt == 0 and offs["k_d"] % d_diff == 0 and offs["v_d"] % d_diff == 0
    aliased = prev is not None
    in_specs = [pl.BlockSpec((rt, d_diff), lambda b, t: (rb0 + b * nt + t, offs["k_d"] // d_diff)),
                pl.BlockSpec((rt, d_diff), lambda b, t: (rb0 + b * nt + t, offs["v_d"] // d_diff)),
                pl.BlockSpec((rt, DIFF_DH), lambda b, t: (t, 0)),
                pl.BlockSpec((rt, DIFF_DH), lambda b, t: (t, 0))]
    args = [z, z, cos, sin]
    aliases = {}
    if aliased:
        in_specs += [pl.BlockSpec(memory_space=pl.ANY)] * 2
        args += list(prev)
        aliases = {4: 0, 5: 1}
    return pl.pallas_call(
        functools.partial(_kv_out_body, rt=rt, heads=heads, aliased=aliased),
        grid=(nb, nt),
        in_specs=in_specs,
        out_specs=[pl.BlockSpec((None, rt * nmap, DIFF_DH), lambda b, t: (layer, b * nt + t, 0)),
                   pl.BlockSpec((None, None, heads, rt, hw), lambda b, t: (layer, b, 0, t, 0))],
        out_shape=[jax.ShapeDtypeStruct((depth, nb * T * nmap, DIFF_DH), F32),
                   jax.ShapeDtypeStruct((depth, nb, heads, T, hw), F32)],
        input_output_aliases=aliases,
        compiler_params=_params(("parallel", "parallel")),
        name="kv_out",
    )(*args)


def _mem_body(q_ref, k_ref, v_ref, o_ref, *, heads, dh):
    scale = dh ** -0.5
    for h in range(heads):
        hs = slice(h * dh, (h + 1) * dh)
        s = lax.dot_general(q_ref[:, hs], k_ref[:, hs].astype(BF16), (((1,), (1,)), ((), ())),
                            preferred_element_type=F32) * scale
        p = _softmax_rows(s).astype(BF16)
        o_ref[:, hs] = jnp.dot(p, v_ref[:, hs].astype(BF16), preferred_element_type=F32).astype(o_ref.dtype)


def _mem_rows_body(q_ref, k_ref, v_ref, o_ref, *, heads, dh, n_mem):
    scale = dh ** -0.5
    nj = dh // LANE
    stride = nj * heads
    for h in range(heads):
        s = None
        for j in range(nj):
            kj = k_ref[pl.ds(j * heads + h, n_mem, stride=stride), :].astype(BF16)
            d = lax.dot_general(q_ref[:, h * dh + j * LANE:h * dh + (j + 1) * LANE], kj, (((1,), (1,)), ((), ())),
                                preferred_element_type=F32)
            s = d if s is None else s + d
        p = _softmax_rows(s * scale).astype(BF16)
        for j in range(nj):
            vj = v_ref[pl.ds(j * heads + h, n_mem, stride=stride), :].astype(BF16)
            o_ref[:, h * dh + j * LANE:h * dh + (j + 1) * LANE] = jnp.dot(
                p, vj, preferred_element_type=F32).astype(o_ref.dtype)


def _mem_attn(q, mem_k, mem_v, *, layer, nb, T, row0, heads, n_mem):
    d = q.shape[1]
    bt = min(T, 512)
    nt = T // bt
    rb0 = row0 // bt
    assert T % bt == 0 and row0 % bt == 0
    if mem_k.shape[3] == d:
        body = functools.partial(_mem_body, heads=heads, dh=d // heads)
    else:
        body = functools.partial(_mem_rows_body, heads=heads, dh=d // heads, n_mem=n_mem)
    kv = pl.BlockSpec((None, None) + mem_k.shape[2:], lambda b, t: (layer, b, 0, 0))
    return pl.pallas_call(
        body,
        grid=(nb, nt),
        in_specs=[pl.BlockSpec((bt, d), lambda b, t: (rb0 + b * nt + t, 0)), kv, kv],
        out_specs=pl.BlockSpec((bt, d), lambda b, t: (b * nt + t, 0)),
        out_shape=jax.ShapeDtypeStruct((nb * T, d), BF16),
        compiler_params=_params(("parallel", "parallel")),
        name="mem_attn",
    )(q, mem_k, mem_v)


def _rope_tables(pos):
    inv = ROPE_THETA ** (-jnp.arange(0, DIFF_DH, 2, dtype=F32) / DIFF_DH)
    ang = pos.astype(F32)[:, None] * inv[None, :]
    cos, sin = jnp.cos(ang), jnp.sin(ang)
    return jnp.concatenate([cos, cos], axis=1), jnp.concatenate([-sin, sin], axis=1)


def kernel(x_prompt, x_sample, cache_diff_k, cache_diff_v, state_gla, state_pool, cache_mem_k, cache_mem_v,
           mem_prompt, w_in, gla_w_a, gla_b_a, gla_norm, pool_w, pool_scale, diff_lambda, diff_norm, w_out,
           w_mq, w_mk, w_mv, w_mo, w_gate, w_up, w_down, ln_g, ln_b):
    Bp, Tp, D = x_prompt.shape
    Bs, Ts, _ = x_sample.shape
    depth = w_in.shape[0]
    past_len = cache_diff_k.shape[2]
    n_mem = mem_prompt.shape[1]
    mem_heads = cache_mem_k.shape[3]
    d_gla = gla_norm.shape[1]
    d_pool = pool_scale.shape[1]
    d_diff = diff_norm.shape[1]
    kw = gla_w_a.shape[2]
    lr = gla_w_a.shape[1]
    d_ff = w_gate.shape[2]
    alpha = (2 * depth) ** 0.25
    Mp, Ms = Bp * Tp, Bs * Ts
    assert lr <= LANE

    n_head = 2 * kw + 2 * d_gla
    offs = {"q_g": 0, "k_g": kw, "v_g": 2 * kw, "r_g": 2 * kw + d_gla,
            "q_d": n_head, "k_d": n_head + d_diff, "v_d": n_head + 2 * d_diff, "u_p": n_head + 3 * d_diff}
    assert d_ff % (2 * LANE) == 0
    hw = 2 * DIFF_DH
    nh = d_diff // hw

    w_main = jnp.concatenate([w_in[:, :, :n_head], w_in[:, :, n_head + lr:]], axis=2).astype(BF16)
    w_alr = jnp.pad(w_in[:, :, n_head:n_head + lr], ((0, 0), (0, 0), (0, LANE - lr))).astype(BF16)
    wa_all = jnp.pad(gla_w_a, ((0, 0), (0, LANE - lr), (0, 0))).astype(BF16)
    pw_all = pool_w.astype(BF16)
    wo_all, wmo_all, wd_all = (w.astype(BF16) for w in (w_out, w_mo, w_down))

    x = (x_prompt.reshape(Mp, D), x_sample.reshape(Ms, D))
    xb = jnp.concatenate(x, axis=0).astype(BF16)
    memb = mem_prompt.reshape(Bp * n_mem, D).astype(BF16)
    cos_p, sin_p = _rope_tables(jnp.arange(Tp))
    cos_s, sin_s = _rope_tables(past_len + jnp.arange(Ts))
    past_k = cache_diff_k.reshape(depth, Bs, past_len * (d_diff // DIFF_DH), DIFF_DH)
    past_v = jnp.transpose(cache_diff_v, (0, 1, 3, 2, 4))
    def mem_rows(c):
        c = c.reshape(depth, Bs, n_mem, mem_heads, D // mem_heads // LANE, LANE)
        return jnp.transpose(c, (0, 1, 2, 4, 3, 5)).reshape(depth, Bs, n_mem * D // LANE, LANE)

    cmem_k = mem_rows(cache_mem_k)
    cmem_v = mem_rows(cache_mem_v)
    hist_s = jnp.pad(state_pool, ((0, 0), (0, 0), (1, 0), (0, 0)))

    outs = {k: [] for k in ("hp", "mk", "mv", "hs")}
    kv_p = kv_s = S_p = S_s = None
    for l in range(depth):
        last = l == depth - 1
        ba = gla_b_a[l].reshape(1, kw)
        gn_g = gla_norm[l].reshape(1, d_gla)
        gn_d = diff_norm[l].reshape(1, d_diff)
        psc = pool_scale[l].reshape(1, d_pool)

        z = _matmul(xb, w_main, layer=l, name="mm_in", out_dtype=F32, bm=1024, bn=1024)
        a_lr = _matmul(xb, w_alr, layer=l, name="mm_alr", out_dtype=F32, bm=1024, bn=LANE)
        kv_p = _kv_out(z, offs, cos_p, sin_p, kv_p, depth=depth, layer=l, nb=Bp, T=Tp, row0=0, d_diff=d_diff)
        kv_s = _kv_out(z, offs, cos_s, sin_s, kv_s, depth=depth, layer=l, nb=Bs, T=Ts, row0=Mp, d_diff=d_diff)
        yg_p, S_p = _gla(z, a_lr, offs, wa_all[l], ba, gn_g, None, S_p, depth=depth, layer=l,
                         nb=Bp, T=Tp, L=CHUNK, row0=0, d_gla=d_gla)
        yg_s, S_s = _gla(z, a_lr, offs, wa_all[l], ba, gn_g, state_gla, S_s, depth=depth, layer=l,
                         nb=Bs, T=Ts, L=Ts, row0=Mp, d_gla=d_gla)
        yp_p, h_p = _pool(z, offs["u_p"], None, pw_all[l], psc, nb=Bp, T=Tp, row0=0, pos0=0, d_pool=d_pool)
        yp_s, h_s = _pool(z, offs["u_p"], hist_s[l], pw_all[l], psc, nb=Bs, T=Ts, row0=Mp, pos0=past_len,
                          d_pool=d_pool)
        yd_p = _diff_prompt(z, offs, cos_p, sin_p, diff_lambda[l], gn_d, nb=Bp, T=Tp, layer=l, d_diff=d_diff)
        yd_s = _diff_sample(z, offs, past_k, past_v, cos_s, sin_s, diff_lambda[l], gn_d,
                            nb=Bs, Ts=Ts, row0=Mp, layer=l, d_diff=d_diff)
        y = _matmul_split([(yg_p, yg_s), (yp_p, yp_s), (yd_p, yd_s)], wo_all, x, layer=l,
                          name="mm_out", alpha=alpha, bm=1024, bn=512)
        x, xb = _layer_norm(y, ln_g[l, 0], ln_b[l, 0])

        mk_p = _matmul(memb, w_mk, layer=l, name="mm_mk", out_dtype=F32, bm=1024, bn=512)
        mv_p = _matmul(memb, w_mv, layer=l, name="mm_mv", out_dtype=F32, bm=1024, bn=512)
        q = _matmul(xb, w_mq, layer=l, name="mm_mq", out_dtype=BF16, bm=1024, bn=512)
        o_p = _mem_attn(q, mk_p.reshape(1, Bp, n_mem, D), mv_p.reshape(1, Bp, n_mem, D),
                        layer=0, nb=Bp, T=Tp, row0=0, heads=mem_heads, n_mem=n_mem)
        o_s = _mem_attn(q, cmem_k, cmem_v, layer=l, nb=Bs, T=Ts, row0=Mp, heads=mem_heads, n_mem=n_mem)
        y = _matmul_split([(o_p, o_s)], wmo_all, x, layer=l, name="mm_mo", alpha=alpha, bm=1024, bn=512)
        x, xb = _layer_norm(y, ln_g[l, 1], ln_b[l, 1])

        hmid = _matmul(xb, w_gate, w2=w_up, layer=l, name="mm_gate_up", out_dtype=BF16, bm=1024, bn=2 * LANE)
        y = _matmul(hmid, wd_all, layer=l, name="mm_down", out_dtype=F32, bm=1024, bn=512, bk=d_ff // 2,
                    res=x, alpha=alpha)
        if last:
            y_prompt, y_sample = _layer_norm(y, ln_g[l, 2], ln_b[l, 2], split_rows=(Mp, Ms))
        else:
            x, xb = _layer_norm(y, ln_g[l, 2], ln_b[l, 2])

        outs["hp"].append(h_p[:, 1:])
        outs["mk"].append(mk_p.reshape(Bp, n_mem, mem_heads, D // mem_heads))
        outs["mv"].append(mv_p.reshape(Bp, n_mem, mem_heads, D // mem_heads))
        outs["hs"].append(h_s[:, 1:])

    st = {k: jnp.stack(v) for k, v in outs.items()}
    new_k_p = kv_p[0].reshape(depth, Bp, Tp, nh, 2, DIFF_DH)
    new_v_p = jnp.transpose(kv_p[1], (0, 1, 3, 2, 4))
    new_k_s = kv_s[0].reshape(depth, Bs, Ts, nh, 2, DIFF_DH)
    new_v_s = jnp.transpose(kv_s[1], (0, 1, 3, 2, 4))
    return (y_prompt.reshape(Bp, Tp, D), y_sample.reshape(Bs, Ts, D),
            new_k_p, new_v_p, S_p, st["hp"], st["mk"], st["mv"],
            new_k_s, new_v_s, S_s, st["hs"])
```

```python
import functools
import math
from typing import NamedTuple

import jax
import jax.numpy as jnp
from jax import lax
from jax.experimental import pallas as pl
from jax.experimental.pallas import tpu as pltpu

F32 = jnp.float32
BF16 = jnp.bfloat16

LANE = 128
V7X_VMEM_BYTES = 64 * 1024 * 1024
VMEM_LIMIT = V7X_VMEM_BYTES - 8 * 1024 * 1024

CHUNK = 64
GLA_HEADS = 4
GLA_TAU = 16.0
POOL_WINDOWS = (2, 4, 8, 16)
POOL_HIST = max(POOL_WINDOWS) - 1
DIFF_DH = 128
ROPE_THETA = 10000.0
NEG_INF = -1e30
LN_EPS = 1e-5
RMS_EPS = 1e-6


def _params(sem):
    return pltpu.CompilerParams(dimension_semantics=sem, vmem_limit_bytes=VMEM_LIMIT)


def _pick(dim, pref):
    if dim <= pref:
        return dim
    b = pref - pref % LANE
    while b >= LANE:
        if dim % b == 0:
            return b
        b -= LANE
    return dim


class _LnRes(NamedTuple):
    y: jax.Array
    mu: jax.Array
    rstd: jax.Array
    g: jax.Array
    b: jax.Array


def _res_inputs(res, bm, bn, ij):
    tile = pl.BlockSpec((bm, bn), lambda *g: ij(*g))
    if not isinstance(res, _LnRes):
        return [tile], [res]
    d = res.g.shape[-1]
    stat = pl.BlockSpec((bm, LANE), lambda *g: (ij(*g)[0], 0))
    vec = pl.BlockSpec((1, bn), lambda *g: (0, ij(*g)[1]))
    return [tile, stat, stat, vec, vec], [res.y, res.mu, res.rstd, res.g.reshape(1, d), res.b.reshape(1, d)]


def _res_value(refs):
    if len(refs) == 1:
        return refs[0][...]
    y, mu, rstd, g, b = refs
    return (y[...] - mu[:, 0:1]) * rstd[:, 0:1] * g[...] + b[...]


def _mm_body(*refs, nk, gated, n_res, alpha, cast_w):
    it = iter(refs)
    x_ref = next(it)
    w_ref = next(it)
    w2_ref = next(it) if gated else None
    res_refs = [next(it) for _ in range(n_res)]
    o_ref = next(it)
    acc_ref = next(it) if nk > 1 else None
    acc2_ref = next(it) if (nk > 1 and gated) else None
    wb_ref = next(it) if cast_w else None
    wb2_ref = next(it) if (cast_w and gated) else None

    def finish(p, p2):
        if gated:
            p = p * (1.0 / (1.0 + jnp.exp(-p))) * p2
        if res_refs:
            p = alpha * _res_value(res_refs) + p
        o_ref[...] = p.astype(o_ref.dtype)

    if cast_w:
        @pl.when(pl.program_id(1) == 0)
        def _():
            wb_ref[...] = w_ref[...].astype(BF16)
            if gated:
                wb2_ref[...] = w2_ref[...].astype(BF16)

        w_ref, w2_ref = wb_ref, wb2_ref

    x = x_ref[...]
    p = jnp.dot(x, w_ref[...], preferred_element_type=F32)
    p2 = jnp.dot(x, w2_ref[...], preferred_element_type=F32) if gated else None
    if nk == 1:
        finish(p, p2)
        return
    k = pl.program_id(2)

    @pl.when(k == 0)
    def _():
        acc_ref[...] = p
        if gated:
            acc2_ref[...] = p2

    @pl.when(k > 0)
    def _():
        acc_ref[...] += p
        if gated:
            acc2_ref[...] += p2

    @pl.when(k == nk - 1)
    def _():
        finish(acc_ref[...], acc2_ref[...] if gated else None)


def _matmul(x, w, *, layer, name, out_dtype, bm, bn, bk=None, w2=None, res=None, alpha=1.0):
    m, kdim = x.shape
    n = w.shape[2]
    bm = _pick(m, bm)
    bn = _pick(n, bn)
    bk = kdim if bk is None else _pick(kdim, bk)
    assert m % bm == 0 and n % bn == 0 and kdim % bk == 0 and w.shape[1] == kdim
    nk = kdim // bk
    gated = w2 is not None
    cast_w = w.dtype == F32
    assert not (cast_w and nk > 1)
    w_spec = pl.BlockSpec((None, bk, bn), lambda j, i, k: (layer, k, j))
    in_specs = [pl.BlockSpec((bm, bk), lambda j, i, k: (i, k)), w_spec]
    args = [x, w]
    if gated:
        in_specs.append(w_spec)
        args.append(w2)
    n_res = 0
    if res is not None:
        res_specs, res_args = _res_inputs(res, bm, bn, lambda j, i, k: (i, j))
        n_res = len(res_args)
        in_specs += res_specs
        args += res_args
    scratch = []
    if nk > 1:
        scratch += [pltpu.VMEM((bm, bn), F32)] * (2 if gated else 1)
    if cast_w:
        scratch += [pltpu.VMEM((bk, bn), BF16)] * (2 if gated else 1)
    return pl.pallas_call(
        functools.partial(_mm_body, nk=nk, gated=gated, n_res=n_res, alpha=alpha, cast_w=cast_w),
        grid=(n // bn, m // bm, nk),
        in_specs=in_specs,
        out_specs=pl.BlockSpec((bm, bn), lambda j, i, k: (i, j)),
        out_shape=jax.ShapeDtypeStruct((m, n), out_dtype),
        scratch_shapes=scratch,
        compiler_params=_params(("parallel", "arbitrary" if cast_w else "parallel", "arbitrary")),
        name=name,
    )(*args)


def _mm_split_body(*refs, widths, nbp, alpha, res_split):
    npc = len(widths)
    p_refs, s_refs = refs[:npc], refs[npc:2 * npc]
    rest = refs[2 * npc:]
    w_ref = rest[0]
    o_ref = rest[-1]
    res_p, res_s = ([rest[1]], [rest[2]]) if res_split else (rest[1:-1], rest[1:-1])
    i = pl.program_id(1)

    def run(lhs_refs, res_refs):
        acc, k0 = None, 0
        for r, wd in zip(lhs_refs, widths):
            d = jnp.dot(r[...], w_ref[k0:k0 + wd, :], preferred_element_type=F32)
            acc = d if acc is None else acc + d
            k0 += wd
        o_ref[...] = (alpha * _res_value(res_refs) + acc).astype(o_ref.dtype)

    @pl.when(i < nbp)
    def _():
        run(p_refs, res_p)

    @pl.when(i >= nbp)
    def _():
        run(s_refs, res_s)


def _matmul_split(pieces, w, res, *, layer, name, alpha, bm, bn):
    mp, ms = pieces[0][0].shape[0], pieces[0][1].shape[0]
    widths = tuple(p.shape[1] for p, _ in pieces)
    _, kdim, n = w.shape
    res_split = isinstance(res, tuple) and not isinstance(res, _LnRes)
    assert sum(widths) == kdim
    bm = math.gcd(math.gcd(mp, ms), bm)
    bn = _pick(n, bn)
    nbp = mp // bm

    def p_map(j, i):
        return (jnp.minimum(i, nbp - 1), 0)

    def s_map(j, i):
        return (jnp.maximum(i - nbp, 0), 0)

    in_specs = [pl.BlockSpec((bm, wd), p_map) for wd in widths] + [pl.BlockSpec((bm, wd), s_map) for wd in widths]
    in_specs.append(pl.BlockSpec((None, kdim, bn), lambda j, i: (layer, 0, j)))
    if res_split:
        in_specs += [pl.BlockSpec((bm, bn), lambda j, i: (jnp.minimum(i, nbp - 1), j)),
                     pl.BlockSpec((bm, bn), lambda j, i: (jnp.maximum(i - nbp, 0), j))]
        res_args = list(res)
    else:
        res_specs, res_args = _res_inputs(res, bm, bn, lambda j, i: (i, j))
        in_specs += res_specs
    return pl.pallas_call(
        functools.partial(_mm_split_body, widths=widths, nbp=nbp, alpha=alpha, res_split=res_split),
        grid=(n // bn, (mp + ms) // bm),
        in_specs=in_specs,
        out_specs=pl.BlockSpec((bm, bn), lambda j, i: (i, j)),
        out_shape=jax.ShapeDtypeStruct((mp + ms, n), F32),
        compiler_params=_params(("parallel", "arbitrary")),
        name=name,
    )(*[p for p, _ in pieces], *[s for _, s in pieces], w, *res_args)


def _ln_body(y_ref, g_ref, b_ref, *o_refs, nbp):
    y = y_ref[...]
    mu = jnp.mean(y, axis=-1, keepdims=True)
    d = y - mu
    var = jnp.mean(d * d, axis=-1, keepdims=True)
    rstd = lax.rsqrt(var + LN_EPS)
    o = d * rstd * g_ref[...] + b_ref[...]
    if nbp is None:
        xb_ref, mu_ref, rstd_ref = o_refs
        xb_ref[...] = o.astype(BF16)
        mu_ref[...] = jnp.broadcast_to(mu, mu_ref.shape)
        rstd_ref[...] = jnp.broadcast_to(rstd, rstd_ref.shape)
        return
    oa_ref, ob_ref = o_refs
    i = pl.program_id(0)

    @pl.when(i < nbp)
    def _():
        oa_ref[...] = o

    @pl.when(i >= nbp)
    def _():
        ob_ref[...] = o


def _layer_norm(y, g, b, *, split_rows=None, bm=256):
    m, d = y.shape
    vec = pl.BlockSpec((1, d), lambda i: (0, 0))
    if split_rows is None:
        bm = _pick(m, bm)
        nbp = None
        row = pl.BlockSpec((bm, d), lambda i: (i, 0))
        stat = pl.BlockSpec((bm, LANE), lambda i: (i, 0))
        out_specs = [row, stat, stat]
        out_shape = [jax.ShapeDtypeStruct((m, d), BF16)] + [jax.ShapeDtypeStruct((m, LANE), F32)] * 2
    else:
        mp, ms = split_rows
        bm = math.gcd(math.gcd(mp, ms), bm)
        nbp = mp // bm
        row = pl.BlockSpec((bm, d), lambda i: (i, 0))
        out_specs = [pl.BlockSpec((bm, d), lambda i: (jnp.minimum(i, nbp - 1), 0)),
                     pl.BlockSpec((bm, d), lambda i: (jnp.maximum(i - nbp, 0), 0))]
        out_shape = [jax.ShapeDtypeStruct((mp, d), F32), jax.ShapeDtypeStruct((ms, d), F32)]
    outs = pl.pallas_call(
        functools.partial(_ln_body, nbp=nbp),
        grid=(m // bm,),
        in_specs=[row, vec, vec],
        out_specs=out_specs,
        out_shape=out_shape,
        compiler_params=_params(("arbitrary",)),
        name="layer_norm",
    )(y, g.reshape(1, d), b.reshape(1, d))
    if split_rows is None:
        xb, mu, rstd = outs
        return xb, _LnRes(y, mu, rstd, g, b)
    return outs


def _gla_body(*refs, heads, dk, dv, L, has_init, aliased, layer, depth):
    it = iter(refs)
    q_ref, k_ref, v_ref, r_ref, a_ref, wa_ref, ba_ref, gn_ref = (next(it) for _ in range(8))
    s0_ref = next(it) if has_init else None
    if aliased:
        next(it)
    y_ref, sout_ref, s_scr = next(it), next(it), next(it)
    c = pl.program_id(1)

    @pl.when(c == 0)
    def _():
        if has_init:
            s_scr[...] = s0_ref[...]
        else:
            s_scr[...] = jnp.zeros_like(s_scr)

    glog = jnp.dot(a_ref[...].astype(BF16), wa_ref[...], preferred_element_type=F32) + ba_ref[...]
    lg = -(jnp.maximum(-glog, 0.0) + jnp.log1p(jnp.exp(-jnp.abs(glog)))) / GLA_TAU
    rows = lax.broadcasted_iota(jnp.int32, (L, L), 0)
    cols = lax.broadcasted_iota(jnp.int32, (L, L), 1)
    causal = rows >= cols
    bcum = jnp.dot(causal.astype(F32), lg, preferred_element_type=F32, precision=lax.Precision.HIGHEST)
    btot_col = lax.dot_general(lg, jnp.ones((L, LANE), F32), (((0,), (0,)), ((), ())),
                               preferred_element_type=F32, precision=lax.Precision.HIGHEST)
    btot_row = bcum[L - 1:L, :]
    qf = q_ref[...] * jnp.exp(bcum) * (dk ** -0.5)
    kf = k_ref[...]
    k_in = kf * jnp.exp(-bcum)
    k_dec = kf * jnp.exp(btot_row - bcum)
    v = v_ref[...]
    r = r_ref[...]
    for h in range(heads):
        ks = slice(h * dk, (h + 1) * dk)
        vs = slice(h * dv, (h + 1) * dv)
        qh = qf[:, ks].astype(BF16)
        vh = v[:, vs].astype(BF16)
        att = lax.dot_general(qh, k_in[:, ks].astype(BF16), (((1,), (1,)), ((), ())), preferred_element_type=F32)
        att = jnp.where(causal, att, 0.0)
        s_old = s_scr[h]
        o = (jnp.dot(qh, s_old.astype(BF16), preferred_element_type=F32)
             + jnp.dot(att.astype(BF16), vh, preferred_element_type=F32))
        o = o * lax.rsqrt(jnp.mean(o * o, axis=-1, keepdims=True) + RMS_EPS) * gn_ref[:, vs]
        rh = r[:, vs]
        y_ref[:, vs] = (o * (rh * (1.0 / (1.0 + jnp.exp(-rh))))).astype(y_ref.dtype)
        decay = jnp.exp(btot_col[ks, :])
        upd = lax.dot_general(k_dec[:, ks].astype(BF16), vh, (((0,), (0,)), ((), ())), preferred_element_type=F32)
        s_new = jnp.concatenate([s_old[:, j * LANE:(j + 1) * LANE] * decay for j in range(dv // LANE)], axis=1) + upd
        s_scr[h] = s_new
    if aliased:
        sout_ref[...] = s_scr[...]
    else:
        for other in range(depth):
            sout_ref[other] = s_scr[...] if other == layer else jnp.zeros(s_scr.shape, F32)


def _gla(z, a_lr, offs, wa, ba, gn, s0, s_prev, *, depth, layer, nb, T, L, row0, d_gla):
    heads = GLA_HEADS
    dv = d_gla // heads
    dk = dv // 2
    kw = heads * dk
    nc = T // L
    rb0 = row0 // L
    has_init = s0 is not None
    assert row0 % L == 0 and dv % LANE == 0
    assert offs["q_g"] == 0 and offs["k_g"] == kw and offs["v_g"] == d_gla and offs["r_g"] == 2 * d_gla

    def rowblk(b, c):
        return rb0 + b * nc + c

    in_specs = [
        pl.BlockSpec((L, kw), lambda b, c: (rowblk(b, c), 0)),
        pl.BlockSpec((L, kw), lambda b, c: (rowblk(b, c), 1)),
        pl.BlockSpec((L, d_gla), lambda b, c: (rowblk(b, c), 1)),
        pl.BlockSpec((L, d_gla), lambda b, c: (rowblk(b, c), 2)),
        pl.BlockSpec((L, LANE), lambda b, c: (rowblk(b, c), 0)),
        pl.BlockSpec((LANE, kw), lambda b, c: (0, 0)),
        pl.BlockSpec((1, kw), lambda b, c: (0, 0)),
        pl.BlockSpec((1, d_gla), lambda b, c: (0, 0)),
    ]
    args = [z, z, z, z, a_lr, wa, ba, gn]
    state_spec = pl.BlockSpec((None, None, heads, dk, dv), lambda b, c: (layer, b, 0, 0, 0))
    if has_init:
        in_specs.append(state_spec)
        args.append(s0)
    aliases = {}
    aliased = s_prev is not None
    if aliased:
        aliases = {len(args): 1}
        in_specs.append(pl.BlockSpec(memory_space=pl.ANY))
        args.append(s_prev)
        sout_spec = state_spec
    else:
        sout_spec = pl.BlockSpec((depth, None, heads, dk, dv), lambda b, c: (0, b, 0, 0, 0))
    return pl.pallas_call(
        functools.partial(_gla_body, heads=heads, dk=dk, dv=dv, L=L, has_init=has_init,
                          aliased=aliased, layer=layer, depth=depth),
        grid=(nb, nc),
        in_specs=in_specs,
        out_specs=[pl.BlockSpec((L, d_gla), lambda b, c: (b * nc + c, 0)), sout_spec],
        out_shape=[jax.ShapeDtypeStruct((nb * T, d_gla), BF16),
                   jax.ShapeDtypeStruct((depth, nb, heads, dk, dv), F32)],
        scratch_shapes=[pltpu.VMEM((heads, dk, dv), F32)],
        input_output_aliases=aliases,
        compiler_params=_params(("parallel", "arbitrary")),
        name="gla_init" if has_init else "gla",
    )(*args)


def _pool_body(*refs, Tt, nt, pos0, has_hist, gw):
    it = iter(refs)
    u_ref = next(it)
    h_ref = next(it) if has_hist else None
    w_ref, sc_ref, y_ref, hnew_ref, full = next(it), next(it), next(it), next(it), next(it)
    t = pl.program_id(1)
    H = POOL_HIST + 1

    @pl.when(t == 0)
    def _():
        if has_hist:
            full[0:H, :] = h_ref[...]
        else:
            full[0:H, :] = jnp.zeros((H, full.shape[1]), F32)

    @pl.when(t > 0)
    def _():
        full[0:H, :] = full[Tt:Tt + H, :]

    full[H:H + Tt, :] = u_ref[...]

    @pl.when(t == nt - 1)
    def _():
        hnew_ref[...] = full[Tt:Tt + H, :]

    pos = pos0 + t * Tt + lax.broadcasted_iota(jnp.int32, (Tt, 1), 0)
    for g, w in enumerate(POOL_WINDOWS):
        cs = slice(g * gw, (g + 1) * gw)
        cur = full[H:H + Tt, cs]
        acc = cur
        for j in range(1, w):
            acc = acc + full[H - j:H - j + Tt, cs]
        cnt = jnp.minimum(pos + 1, w).astype(F32)
        pooled = acc / cnt - cur
        y = jnp.dot(pooled.astype(BF16), w_ref[g], preferred_element_type=F32) * sc_ref[:, cs]
        y_ref[:, cs] = y.astype(y_ref.dtype)


def _pool(z, off_u, hist, w_pool, scale, *, nb, T, row0, pos0, d_pool):
    Tt = min(T, 512)
    assert T % Tt == 0 and row0 % Tt == 0 and off_u % d_pool == 0 and Tt >= POOL_HIST + 1
    nt = T // Tt
    rb0 = row0 // Tt
    ng = len(POOL_WINDOWS)
    gw = d_pool // ng
    has_hist = hist is not None
    in_specs = [pl.BlockSpec((Tt, d_pool), lambda b, t: (rb0 + b * nt + t, off_u // d_pool))]
    args = [z]
    if has_hist:
        in_specs.append(pl.BlockSpec((None, POOL_HIST + 1, d_pool), lambda b, t: (b, 0, 0)))
        args.append(hist)
    in_specs += [pl.BlockSpec((ng, gw, gw), lambda b, t: (0, 0, 0)),
                 pl.BlockSpec((1, d_pool), lambda b, t: (0, 0))]
    args += [w_pool, scale]
    return pl.pallas_call(
        functools.partial(_pool_body, Tt=Tt, nt=nt, pos0=pos0, has_hist=has_hist, gw=gw),
        grid=(nb, nt),
        in_specs=in_specs,
        out_specs=[pl.BlockSpec((Tt, d_pool), lambda b, t: (b * nt + t, 0)),
                   pl.BlockSpec((None, POOL_HIST + 1, d_pool), lambda b, t: (b, 0, 0))],
        out_shape=[jax.ShapeDtypeStruct((nb * T, d_pool), BF16),
                   jax.ShapeDtypeStruct((nb, POOL_HIST + 1, d_pool), F32)],
        scratch_shapes=[pltpu.VMEM((Tt + POOL_HIST + 1, d_pool), F32)],
        compiler_params=_params(("parallel", "arbitrary")),
        name="pool_hist" if has_hist else "pool",
    )(*args)


def _lam_init(layer):
    return 0.8 - 0.6 * math.exp(-0.3 * layer)


def _lam(lp_ref, layer):
    lp = lp_ref[...]
    a = jnp.sum(lp[0:1, :] * lp[1:2, :], axis=-1, keepdims=True)
    b = jnp.sum(lp[2:3, :] * lp[3:4, :], axis=-1, keepdims=True)
    return jnp.exp(a) - jnp.exp(b) + _lam_init(layer)


def _rope(x, cos, sin_signed):
    outs = []
    for m in range(2):
        xm = x[:, m * DIFF_DH:(m + 1) * DIFF_DH]
        outs.append(xm * cos + pltpu.roll(xm, DIFF_DH // 2, 1) * sin_signed)
    return jnp.concatenate(outs, axis=1)


LOG2_E = 1.4426950408889634


def _softmax_parts(s, scale):
    m = jnp.max(s, axis=-1, keepdims=True)
    e = jnp.exp2((s - m) * (scale * LOG2_E))
    return e, 1.0 / jnp.sum(e, axis=-1, keepdims=True)


def _diff_finish(o, gn, layer):
    o = o * lax.rsqrt(jnp.mean(o * o, axis=-1, keepdims=True) + RMS_EPS) * gn
    return o * (1.0 - _lam_init(layer))


def _diffp_body(q_ref, k_ref, v_ref, cos_ref, sin_ref, lp_ref, gn_ref, y_ref, qb, kb, vb, *, T, bq, layer):
    dh = DIFF_DH
    lam = _lam(lp_ref, layer)
    cos = cos_ref[...]
    sin = sin_ref[...]
    kb[...] = _rope(k_ref[...], cos, sin).astype(BF16)
    qb[...] = _rope(q_ref[...], cos, sin).astype(BF16)
    vb[...] = v_ref[...].astype(BF16)
    scale = dh ** -0.5
    for i in range(T // bq):
        n = (i + 1) * bq
        qs = slice(i * bq, n)
        qc = (i * bq + lax.broadcasted_iota(jnp.int32, (bq, n), 0)) // CHUNK
        kc = lax.broadcasted_iota(jnp.int32, (bq, n), 1) // CHUNK
        mask = kc <= qc
        es, inv = [], []
        for m in range(2):
            ms = slice(m * dh, (m + 1) * dh)
            s = lax.dot_general(qb[qs, ms], kb[0:n, ms], (((1,), (1,)), ((), ())), preferred_element_type=F32)
            e, r = _softmax_parts(jnp.where(mask, s, NEG_INF), scale)
            es.append(e)
            inv.append(r)
        p = (es[0] * inv[0] - es[1] * (lam * inv[1])).astype(BF16)
        o = jnp.dot(p, vb[0:n, :], preferred_element_type=F32)
        y_ref[qs, :] = _diff_finish(o, gn_ref[...], layer).astype(y_ref.dtype)


def _diff_prompt(z, offs, cos, sin, lp, gn, *, nb, T, layer, d_diff):
    hw = 2 * DIFF_DH
    heads = d_diff // hw
    bq = min(T, 256)
    assert T % bq == 0 and bq % CHUNK == 0
    for name in ("q_d", "k_d", "v_d"):
        assert offs[name] % hw == 0

    def col(name):
        return lambda b, h: (b, offs[name] // hw + h)

    tab = pl.BlockSpec((T, DIFF_DH), lambda b, h: (0, 0))
    return pl.pallas_call(
        functools.partial(_diffp_body, T=T, bq=bq, layer=layer),
        grid=(nb, heads),
        in_specs=[pl.BlockSpec((T, hw), col("q_d")), pl.BlockSpec((T, hw), col("k_d")),
                  pl.BlockSpec((T, hw), col("v_d")), tab, tab,
                  pl.BlockSpec((4, DIFF_DH), lambda b, h: (0, 0)),
                  pl.BlockSpec((1, hw), lambda b, h: (0, h))],
        out_specs=pl.BlockSpec((T, hw), lambda b, h: (b, h)),
        out_shape=jax.ShapeDtypeStruct((nb * T, d_diff), BF16),
        scratch_shapes=[pltpu.VMEM((T, hw), BF16)] * 3,
        compiler_params=_params(("parallel", "parallel")),
        name="diff_prompt",
    )(z, z, z, cos, sin, lp, gn)


def _diffs_body(q_ref, k_ref, v_ref, pk_ref, pv_ref, cos_ref, sin_ref, lp_ref, gn_ref, y_ref,
                qb, s_past, s_new, p_past, acc, *, Ts, kb, nkb, heads, layer):
    dh = DIFF_DH
    hw = 2 * dh
    nmap = 2 * heads
    j = pl.program_id(1)
    scale = dh ** -0.5

    @pl.when(j == 0)
    def _():
        cos = cos_ref[...]
        sin = sin_ref[...]
        for c in range(nmap):
            cs = slice(c * dh, (c + 1) * dh)
            kc = k_ref[:, cs]
            kc = kc * cos + pltpu.roll(kc, dh // 2, 1) * sin
            qc = q_ref[:, cs]
            qc = (qc * cos + pltpu.roll(qc, dh // 2, 1) * sin).astype(BF16)
            qb[:, cs] = qc
            s_new[c] = lax.dot_general(qc, kc.astype(BF16), (((1,), (1,)), ((), ())),
                                       preferred_element_type=F32) * scale

    @pl.when(j < nkb)
    def _():
        for c in range(nmap):
            pk = pk_ref[pl.ds(c, kb, stride=nmap), :].astype(BF16)
            s_past[c, j] = lax.dot_general(qb[:, c * dh:(c + 1) * dh], pk, (((1,), (1,)), ((), ())),
                                           preferred_element_type=F32) * scale

    @pl.when(j == nkb)
    def _():
        lam = _lam(lp_ref, layer)
        for h in range(heads):
            es, en, inv = [], [], []
            for m in range(2):
                sp = s_past[2 * h + m]
                sn = s_new[2 * h + m]
                mx = jnp.maximum(jnp.max(jnp.max(sp, axis=0), axis=-1, keepdims=True),
                                 jnp.max(sn, axis=-1, keepdims=True))
                ep = jnp.exp(sp - mx[None])
                e_n = jnp.exp(sn - mx)
                tot = jnp.sum(jnp.sum(ep, axis=0), axis=-1, keepdims=True) + jnp.sum(e_n, axis=-1, keepdims=True)
                es.append(ep)
                en.append(e_n)
                inv.append(1.0 / tot)
            p_past[h] = (es[0] * inv[0][None] - lam * (es[1] * inv[1][None])).astype(BF16)
            pn = (en[0] * inv[0] - lam * (en[1] * inv[1])).astype(BF16)
            hs = slice(h * hw, (h + 1) * hw)
            acc[:, hs] = jnp.dot(pn, v_ref[:, hs].astype(BF16), preferred_element_type=F32)

    @pl.when(j >= nkb)
    def _():
        for h in range(heads):
            hs = slice(h * hw, (h + 1) * hw)
            acc[:, hs] += jnp.dot(p_past[h, j - nkb], pv_ref[h].astype(BF16), preferred_element_type=F32)

    @pl.when(j == 2 * nkb - 1)
    def _():
        for h in range(heads):
            hs = slice(h * hw, (h + 1) * hw)
            y_ref[:, hs] = _diff_finish(acc[:, hs], gn_ref[:, hs], layer).astype(y_ref.dtype)


def _diff_sample(z, offs, past_k, past_v, cos, sin, lp, gn, *, nb, Ts, row0, layer, d_diff):
    hw = 2 * DIFF_DH
    heads = d_diff // hw
    nmap = 2 * heads
    P = past_v.shape[3]
    kb = _pick(P, 1024)
    nkb = P // kb
    rb0 = row0 // Ts
    assert row0 % Ts == 0 and P % kb == 0
    for name in ("q_d", "k_d", "v_d"):
        assert offs[name] % d_diff == 0

    def col(name):
        return lambda b, j: (rb0 + b, offs[name] // d_diff)

    tab = pl.BlockSpec((Ts, DIFF_DH), lambda b, j: (0, 0))
    return pl.pallas_call(
        functools.partial(_diffs_body, Ts=Ts, kb=kb, nkb=nkb, heads=heads, layer=layer),
        grid=(nb, 2 * nkb),
        in_specs=[pl.BlockSpec((Ts, d_diff), col("q_d")), pl.BlockSpec((Ts, d_diff), col("k_d")),
                  pl.BlockSpec((Ts, d_diff), col("v_d")),
                  pl.BlockSpec((None, None, kb * nmap, DIFF_DH), lambda b, j: (layer, b, jnp.minimum(j, nkb - 1), 0)),
                  pl.BlockSpec((None, None, heads, kb, hw), lambda b, j: (layer, b, 0, jnp.maximum(j - nkb, 0), 0)),
                  tab, tab,
                  pl.BlockSpec((4, DIFF_DH), lambda b, j: (0, 0)),
                  pl.BlockSpec((1, d_diff), lambda b, j: (0, 0))],
        out_specs=pl.BlockSpec((Ts, d_diff), lambda b, j: (b, 0)),
        out_shape=jax.ShapeDtypeStruct((nb * Ts, d_diff), BF16),
        scratch_shapes=[pltpu.VMEM((Ts, d_diff), BF16),
                        pltpu.VMEM((nmap, nkb, Ts, kb), F32),
                        pltpu.VMEM((nmap, Ts, Ts), F32),
                        pltpu.VMEM((heads, nkb, Ts, kb), BF16),
                        pltpu.VMEM((Ts, d_diff), F32)],
        compiler_params=_params(("parallel", "arbitrary")),
        name="diff_sample",
    )(z, z, z, past_k, past_v, cos, sin, lp, gn)


def _kv_out_body(*refs, rt, heads, layer, depth, aliased):
    k_ref, v_ref, cos_ref, sin_ref = refs[:4]
    ko_ref, vo_ref = refs[-2:]
    dh = DIFF_DH
    nmap = 2 * heads
    if not aliased:
        for other in range(depth):
            if other != layer:
                ko_ref[other] = jnp.zeros(ko_ref.shape[1:], F32)
                vo_ref[other] = jnp.zeros(vo_ref.shape[1:], F32)
        ko_ref, vo_ref = ko_ref.at[layer], vo_ref.at[layer]
    cos = cos_ref[...]
    sin = sin_ref[...]
    for c in range(nmap):
        kc = k_ref[:, c * dh:(c + 1) * dh]
        ko_ref[pl.ds(c, rt, stride=nmap), :] = kc * cos + pltpu.roll(kc, dh // 2, 1) * sin
    for h in range(heads):
        vo_ref[h] = v_ref[:, h * 2 * dh:(h + 1) * 2 * dh]


def _kv_out(z, offs, cos, sin, prev, *, depth, layer, nb, T, row0, d_diff):
    hw = 2 * DIFF_DH
    heads = d_diff // hw
    nmap = 2 * heads
    rt = min(T, 512)
    nt = T // rt
    rb0 = row0 // rt
    assert T % rt == 0 and row0 % rt == 0 and offs["k_d"] % d_diff == 0 and offs["v_d"] % d_diff == 0
    aliased = prev is not None
    in_specs = [pl.BlockSpec((rt, d_diff), lambda b, t: (rb0 + b * nt + t, offs["k_d"] // d_diff)),
                pl.BlockSpec((rt, d_diff), lambda b, t: (rb0 + b * nt + t, offs["v_d"] // d_diff)),
                pl.BlockSpec((rt, DIFF_DH), lambda b, t: (t, 0)),
                pl.BlockSpec((rt, DIFF_DH), lambda b, t: (t, 0))]
    args = [z, z, cos, sin]
    aliases = {}
    if aliased:
        in_specs += [pl.BlockSpec(memory_space=pl.ANY)] * 2
        args += list(prev)
        aliases = {4: 0, 5: 1}
    lead, l_idx = (None, layer) if aliased else (depth, 0)
    return pl.pallas_call(
        functools.partial(_kv_out_body, rt=rt, heads=heads, layer=layer, depth=depth, aliased=aliased),
        grid=(nb, nt),
        in_specs=in_specs,
        out_specs=[pl.BlockSpec((lead, rt * nmap, DIFF_DH), lambda b, t: (l_idx, b * nt + t, 0)),
                   pl.BlockSpec((lead, None, heads, rt, hw), lambda b, t: (l_idx, b, 0, t, 0))],
        out_shape=[jax.ShapeDtypeStruct((depth, nb * T * nmap, DIFF_DH), F32),
                   jax.ShapeDtypeStruct((depth, nb, heads, T, hw), F32)],
        input_output_aliases=aliases,
        compiler_params=_params(("parallel", "parallel")),
        name="kv_out",
    )(*args)


def _mem_body(q_ref, k_ref, v_ref, o_ref, *, heads, dh):
    scale = dh ** -0.5
    for h in range(heads):
        hs = slice(h * dh, (h + 1) * dh)
        s = lax.dot_general(q_ref[:, hs], k_ref[:, hs].astype(BF16), (((1,), (1,)), ((), ())),
                            preferred_element_type=F32)
        e, r = _softmax_parts(s, scale)
        p = (e * r).astype(BF16)
        o_ref[:, hs] = jnp.dot(p, v_ref[:, hs].astype(BF16), preferred_element_type=F32).astype(o_ref.dtype)


def _mem_rows_body(q_ref, k_ref, v_ref, o_ref, *, heads, dh, n_mem):
    scale = dh ** -0.5
    nj = dh // LANE
    stride = nj * heads
    for h in range(heads):
        s = None
        for j in range(nj):
            kj = k_ref[pl.ds(j * heads + h, n_mem, stride=stride), :].astype(BF16)
            d = lax.dot_general(q_ref[:, h * dh + j * LANE:h * dh + (j + 1) * LANE], kj, (((1,), (1,)), ((), ())),
                                preferred_element_type=F32)
            s = d if s is None else s + d
        e, r = _softmax_parts(s, scale)
        p = (e * r).astype(BF16)
        for j in range(nj):
            vj = v_ref[pl.ds(j * heads + h, n_mem, stride=stride), :].astype(BF16)
            o_ref[:, h * dh + j * LANE:h * dh + (j + 1) * LANE] = jnp.dot(
                p, vj, preferred_element_type=F32).astype(o_ref.dtype)


def _mem_attn(q, mem_k, mem_v, *, layer, nb, T, row0, heads, n_mem):
    d = q.shape[1]
    bt = min(T, 512)
    nt = T // bt
    rb0 = row0 // bt
    assert T % bt == 0 and row0 % bt == 0
    if mem_k.shape[3] == d:
        body = functools.partial(_mem_body, heads=heads, dh=d // heads)
    else:
        body = functools.partial(_mem_rows_body, heads=heads, dh=d // heads, n_mem=n_mem)
    kv = pl.BlockSpec((None, None) + mem_k.shape[2:], lambda b, t: (layer, b, 0, 0))
    return pl.pallas_call(
        body,
        grid=(nb, nt),
        in_specs=[pl.BlockSpec((bt, d), lambda b, t: (rb0 + b * nt + t, 0)), kv, kv],
        out_specs=pl.BlockSpec((bt, d), lambda b, t: (b * nt + t, 0)),
        out_shape=jax.ShapeDtypeStruct((nb * T, d), BF16),
        compiler_params=_params(("parallel", "parallel")),
        name="mem_attn",
    )(q, mem_k, mem_v)


def _rope_tables(pos):
    inv = ROPE_THETA ** (-jnp.arange(0, DIFF_DH, 2, dtype=F32) / DIFF_DH)
    ang = pos.astype(F32)[:, None] * inv[None, :]
    cos, sin = jnp.cos(ang), jnp.sin(ang)
    return jnp.concatenate([cos, cos], axis=1), jnp.concatenate([-sin, sin], axis=1)


def kernel(x_prompt, x_sample, cache_diff_k, cache_diff_v, state_gla, state_pool, cache_mem_k, cache_mem_v,
           mem_prompt, w_in, gla_w_a, gla_b_a, gla_norm, pool_w, pool_scale, diff_lambda, diff_norm, w_out,
           w_mq, w_mk, w_mv, w_mo, w_gate, w_up, w_down, ln_g, ln_b):
    Bp, Tp, D = x_prompt.shape
    Bs, Ts, _ = x_sample.shape
    depth = w_in.shape[0]
    past_len = cache_diff_k.shape[2]
    n_mem = mem_prompt.shape[1]
    mem_heads = cache_mem_k.shape[3]
    d_gla = gla_norm.shape[1]
    d_pool = pool_scale.shape[1]
    d_diff = diff_norm.shape[1]
    kw = gla_w_a.shape[2]
    lr = gla_w_a.shape[1]
    d_ff = w_gate.shape[2]
    alpha = (2 * depth) ** 0.25
    Mp, Ms = Bp * Tp, Bs * Ts
    assert lr <= LANE

    n_head = 2 * kw + 2 * d_gla
    offs = {"q_g": 0, "k_g": kw, "v_g": 2 * kw, "r_g": 2 * kw + d_gla,
            "q_d": n_head, "k_d": n_head + d_diff, "v_d": n_head + 2 * d_diff, "u_p": n_head + 3 * d_diff}
    assert d_ff % (2 * LANE) == 0
    hw = 2 * DIFF_DH
    nh = d_diff // hw

    w_main = jnp.concatenate([w_in[:, :, :n_head], w_in[:, :, n_head + lr:]], axis=2).astype(BF16)
    w_alr = jnp.pad(w_in[:, :, n_head:n_head + lr], ((0, 0), (0, 0), (0, LANE - lr))).astype(BF16)
    wa_all = jnp.pad(gla_w_a, ((0, 0), (0, LANE - lr), (0, 0))).astype(BF16)
    pw_all = pool_w.astype(BF16)
    wo_all, wmo_all, wd_all = (w.astype(BF16) for w in (w_out, w_mo, w_down))

    x = (x_prompt.reshape(Mp, D), x_sample.reshape(Ms, D))
    xb = jnp.concatenate(x, axis=0).astype(BF16)
    memb = mem_prompt.reshape(Bp * n_mem, D).astype(BF16)
    cos_p, sin_p = _rope_tables(jnp.arange(Tp))
    cos_s, sin_s = _rope_tables(past_len + jnp.arange(Ts))
    past_k = cache_diff_k.reshape(depth, Bs, past_len * (d_diff // DIFF_DH), DIFF_DH)
    past_v = jnp.transpose(cache_diff_v, (0, 1, 3, 2, 4))
    def mem_rows(c):
        c = c.reshape(depth, Bs, n_mem, mem_heads, D // mem_heads // LANE, LANE)
        return jnp.transpose(c, (0, 1, 2, 4, 3, 5)).reshape(depth, Bs, n_mem * D // LANE, LANE)

    cmem_k = mem_rows(cache_mem_k)
    cmem_v = mem_rows(cache_mem_v)
    hist_s = jnp.pad(state_pool, ((0, 0), (0, 0), (1, 0), (0, 0)))

    outs = {k: [] for k in ("hp", "mk", "mv", "hs")}
    kv_p = kv_s = S_p = S_s = None
    for l in range(depth):
        last = l == depth - 1
        ba = gla_b_a[l].reshape(1, kw)
        gn_g = gla_norm[l].reshape(1, d_gla)
        gn_d = diff_norm[l].reshape(1, d_diff)
        psc = pool_scale[l].reshape(1, d_pool)

        z = _matmul(xb, w_main, layer=l, name="mm_in", out_dtype=F32, bm=1024, bn=1024)
        a_lr = _matmul(xb, w_alr, layer=l, name="mm_alr", out_dtype=F32, bm=1024, bn=LANE)
        kv_p = _kv_out(z, offs, cos_p, sin_p, kv_p, depth=depth, layer=l, nb=Bp, T=Tp, row0=0, d_diff=d_diff)
        kv_s = _kv_out(z, offs, cos_s, sin_s, kv_s, depth=depth, layer=l, nb=Bs, T=Ts, row0=Mp, d_diff=d_diff)
        yg_p, S_p = _gla(z, a_lr, offs, wa_all[l], ba, gn_g, None, S_p, depth=depth, layer=l,
                         nb=Bp, T=Tp, L=CHUNK, row0=0, d_gla=d_gla)
        yg_s, S_s = _gla(z, a_lr, offs, wa_all[l], ba, gn_g, state_gla, S_s, depth=depth, layer=l,
                         nb=Bs, T=Ts, L=Ts, row0=Mp, d_gla=d_gla)
        yp_p, h_p = _pool(z, offs["u_p"], None, pw_all[l], psc, nb=Bp, T=Tp, row0=0, pos0=0, d_pool=d_pool)
        yp_s, h_s = _pool(z, offs["u_p"], hist_s[l], pw_all[l], psc, nb=Bs, T=Ts, row0=Mp, pos0=past_len,
                          d_pool=d_pool)
        yd_p = _diff_prompt(z, offs, cos_p, sin_p, diff_lambda[l], gn_d, nb=Bp, T=Tp, layer=l, d_diff=d_diff)
        yd_s = _diff_sample(z, offs, past_k, past_v, cos_s, sin_s, diff_lambda[l], gn_d,
                            nb=Bs, Ts=Ts, row0=Mp, layer=l, d_diff=d_diff)
        y = _matmul_split([(yg_p, yg_s), (yp_p, yp_s), (yd_p, yd_s)], wo_all, x, layer=l,
                          name="mm_out", alpha=alpha, bm=1024, bn=512)
        xb, x = _layer_norm(y, ln_g[l, 0], ln_b[l, 0])

        mk_p = _matmul(memb, w_mk, layer=l, name="mm_mk", out_dtype=F32, bm=1024, bn=512)
        mv_p = _matmul(memb, w_mv, layer=l, name="mm_mv", out_dtype=F32, bm=1024, bn=512)
        q = _matmul(xb, w_mq, layer=l, name="mm_mq", out_dtype=BF16, bm=1024, bn=512)
        o_p = _mem_attn(q, mk_p.reshape(1, Bp, n_mem, D), mv_p.reshape(1, Bp, n_mem, D),
                        layer=0, nb=Bp, T=Tp, row0=0, heads=mem_heads, n_mem=n_mem)
        o_s = _mem_attn(q, cmem_k, cmem_v, layer=l, nb=Bs, T=Ts, row0=Mp, heads=mem_heads, n_mem=n_mem)
        y = _matmul_split([(o_p, o_s)], wmo_all, x, layer=l, name="mm_mo", alpha=alpha, bm=1024, bn=512)
        xb, x = _layer_norm(y, ln_g[l, 1], ln_b[l, 1])

        hmid = _matmul(xb, w_gate, w2=w_up, layer=l, name="mm_gate_up", out_dtype=BF16, bm=1024, bn=2 * LANE)
        y = _matmul(hmid, wd_all, layer=l, name="mm_down", out_dtype=F32, bm=512, bn=512, res=x, alpha=alpha)
        if last:
            y_prompt, y_sample = _layer_norm(y, ln_g[l, 2], ln_b[l, 2], split_rows=(Mp, Ms))
        else:
            xb, x = _layer_norm(y, ln_g[l, 2], ln_b[l, 2])

        outs["hp"].append(h_p[:, 1:])
        outs["mk"].append(mk_p.reshape(Bp, n_mem, mem_heads, D // mem_heads))
        outs["mv"].append(mv_p.reshape(Bp, n_mem, mem_heads, D // mem_heads))
        outs["hs"].append(h_s[:, 1:])

    st = {k: jnp.stack(v) for k, v in outs.items()}
    new_k_p = kv_p[0].reshape(depth, Bp, Tp, nh, 2, DIFF_DH)
    new_v_p = jnp.transpose(kv_p[1], (0, 1, 3, 2, 4))
    new_k_s = kv_s[0].reshape(depth, Bs, Ts, nh, 2, DIFF_DH)
    new_v_s = jnp.transpose(kv_s[1], (0, 1, 3, 2, 4))
    return (y_prompt.reshape(Bp, Tp, D), y_sample.reshape(Bs, Ts, D),
            new_k_p, new_v_p, S_p, st["hp"], st["mk"], st["mv"],
            new_k_s, new_v_s, S_s, st["hs"])
```

```python
import functools
import math
from typing import NamedTuple

import jax
import jax.numpy as jnp
from jax import lax
from jax.experimental import pallas as pl
from jax.experimental.pallas import tpu as pltpu

F32 = jnp.float32
BF16 = jnp.bfloat16

LANE = 128
V7X_VMEM_BYTES = 64 * 1024 * 1024
VMEM_LIMIT = V7X_VMEM_BYTES - 8 * 1024 * 1024

CHUNK = 64
GLA_HEADS = 4
GLA_TAU = 16.0
POOL_WINDOWS = (2, 4, 8, 16)
POOL_HIST = max(POOL_WINDOWS) - 1
DIFF_DH = 128
ROPE_THETA = 10000.0
NEG_INF = -1e30
LN_EPS = 1e-5
RMS_EPS = 1e-6


def _params(sem):
    return pltpu.CompilerParams(dimension_semantics=sem, vmem_limit_bytes=VMEM_LIMIT)


def _pick(dim, pref):
    if dim <= pref:
        return dim
    b = pref - pref % LANE
    while b >= LANE:
        if dim % b == 0:
            return b
        b -= LANE
    return dim


class _LnRes(NamedTuple):
    y: jax.Array
    mu: jax.Array
    rstd: jax.Array
    g: jax.Array
    b: jax.Array


def _res_inputs(res, bm, bn, ij):
    tile = pl.BlockSpec((bm, bn), lambda *g: ij(*g))
    if not isinstance(res, _LnRes):
        return [tile], [res]
    d = res.g.shape[-1]
    stat = pl.BlockSpec((bm, LANE), lambda *g: (ij(*g)[0], 0))
    vec = pl.BlockSpec((1, bn), lambda *g: (0, ij(*g)[1]))
    return [tile, stat, stat, vec, vec], [res.y, res.mu, res.rstd, res.g.reshape(1, d), res.b.reshape(1, d)]


def _res_value(refs):
    if len(refs) == 1:
        return refs[0][...]
    y, mu, rstd, g, b = refs
    return (y[...] - mu[:, 0:1]) * rstd[:, 0:1] * g[...] + b[...]


def _mm_body(*refs, nk, gated, n_res, alpha, cast_w):
    it = iter(refs)
    x_ref = next(it)
    w_ref = next(it)
    w2_ref = next(it) if gated else None
    res_refs = [next(it) for _ in range(n_res)]
    o_ref = next(it)
    acc_ref = next(it) if nk > 1 else None
    acc2_ref = next(it) if (nk > 1 and gated) else None
    wb_ref = next(it) if cast_w else None
    wb2_ref = next(it) if (cast_w and gated) else None

    def finish(p, p2):
        if gated:
            p = p * (1.0 / (1.0 + jnp.exp(-p))) * p2
        if res_refs:
            p = alpha * _res_value(res_refs) + p
        o_ref[...] = p.astype(o_ref.dtype)

    if cast_w:
        @pl.when(pl.program_id(1) == 0)
        def _():
            wb_ref[...] = w_ref[...].astype(BF16)
            if gated:
                wb2_ref[...] = w2_ref[...].astype(BF16)

        w_ref, w2_ref = wb_ref, wb2_ref

    x = x_ref[...]
    p = jnp.dot(x, w_ref[...], preferred_element_type=F32)
    p2 = jnp.dot(x, w2_ref[...], preferred_element_type=F32) if gated else None
    if nk == 1:
        finish(p, p2)
        return
    k = pl.program_id(2)

    @pl.when(k == 0)
    def _():
        acc_ref[...] = p
        if gated:
            acc2_ref[...] = p2

    @pl.when(k > 0)
    def _():
        acc_ref[...] += p
        if gated:
            acc2_ref[...] += p2

    @pl.when(k == nk - 1)
    def _():
        finish(acc_ref[...], acc2_ref[...] if gated else None)


def _matmul(x, w, *, layer, name, out_dtype, bm, bn, bk=None, w2=None, res=None, alpha=1.0):
    m, kdim = x.shape
    n = w.shape[2]
    bm = _pick(m, bm)
    bn = _pick(n, bn)
    bk = kdim if bk is None else _pick(kdim, bk)
    assert m % bm == 0 and n % bn == 0 and kdim % bk == 0 and w.shape[1] == kdim
    nk = kdim // bk
    gated = w2 is not None
    cast_w = w.dtype == F32
    assert not (cast_w and nk > 1)
    w_spec = pl.BlockSpec((None, bk, bn), lambda j, i, k: (layer, k, j))
    in_specs = [pl.BlockSpec((bm, bk), lambda j, i, k: (i, k)), w_spec]
    args = [x, w]
    if gated:
        in_specs.append(w_spec)
        args.append(w2)
    n_res = 0
    if res is not None:
        res_specs, res_args = _res_inputs(res, bm, bn, lambda j, i, k: (i, j))
        n_res = len(res_args)
        in_specs += res_specs
        args += res_args
    scratch = []
    if nk > 1:
        scratch += [pltpu.VMEM((bm, bn), F32)] * (2 if gated else 1)
    if cast_w:
        scratch += [pltpu.VMEM((bk, bn), BF16)] * (2 if gated else 1)
    return pl.pallas_call(
        functools.partial(_mm_body, nk=nk, gated=gated, n_res=n_res, alpha=alpha, cast_w=cast_w),
        grid=(n // bn, m // bm, nk),
        in_specs=in_specs,
        out_specs=pl.BlockSpec((bm, bn), lambda j, i, k: (i, j)),
        out_shape=jax.ShapeDtypeStruct((m, n), out_dtype),
        scratch_shapes=scratch,
        compiler_params=_params(("parallel", "arbitrary" if cast_w else "parallel", "arbitrary")),
        name=name,
    )(*args)


def _mm_split_body(*refs, widths, nbp, alpha, res_split):
    npc = len(widths)
    p_refs, s_refs = refs[:npc], refs[npc:2 * npc]
    rest = refs[2 * npc:]
    w_ref = rest[0]
    o_ref = rest[-1]
    res_p, res_s = ([rest[1]], [rest[2]]) if res_split else (rest[1:-1], rest[1:-1])
    i = pl.program_id(1)

    def run(lhs_refs, res_refs):
        acc, k0 = None, 0
        for r, wd in zip(lhs_refs, widths):
            d = jnp.dot(r[...], w_ref[k0:k0 + wd, :], preferred_element_type=F32)
            acc = d if acc is None else acc + d
            k0 += wd
        o_ref[...] = (alpha * _res_value(res_refs) + acc).astype(o_ref.dtype)

    @pl.when(i < nbp)
    def _():
        run(p_refs, res_p)

    @pl.when(i >= nbp)
    def _():
        run(s_refs, res_s)


def _matmul_split(pieces, w, res, *, layer, name, alpha, bm, bn):
    mp, ms = pieces[0][0].shape[0], pieces[0][1].shape[0]
    widths = tuple(p.shape[1] for p, _ in pieces)
    _, kdim, n = w.shape
    res_split = isinstance(res, tuple) and not isinstance(res, _LnRes)
    assert sum(widths) == kdim
    bm = math.gcd(math.gcd(mp, ms), bm)
    bn = _pick(n, bn)
    nbp = mp // bm

    def p_map(j, i):
        return (jnp.minimum(i, nbp - 1), 0)

    def s_map(j, i):
        return (jnp.maximum(i - nbp, 0), 0)

    in_specs = [pl.BlockSpec((bm, wd), p_map) for wd in widths] + [pl.BlockSpec((bm, wd), s_map) for wd in widths]
    in_specs.append(pl.BlockSpec((None, kdim, bn), lambda j, i: (layer, 0, j)))
    if res_split:
        in_specs += [pl.BlockSpec((bm, bn), lambda j, i: (jnp.minimum(i, nbp - 1), j)),
                     pl.BlockSpec((bm, bn), lambda j, i: (jnp.maximum(i - nbp, 0), j))]
        res_args = list(res)
    else:
        res_specs, res_args = _res_inputs(res, bm, bn, lambda j, i: (i, j))
        in_specs += res_specs
    return pl.pallas_call(
        functools.partial(_mm_split_body, widths=widths, nbp=nbp, alpha=alpha, res_split=res_split),
        grid=(n // bn, (mp + ms) // bm),
        in_specs=in_specs,
        out_specs=pl.BlockSpec((bm, bn), lambda j, i: (i, j)),
        out_shape=jax.ShapeDtypeStruct((mp + ms, n), F32),
        compiler_params=_params(("parallel", "arbitrary")),
        name=name,
    )(*[p for p, _ in pieces], *[s for _, s in pieces], w, *res_args)


def _ln_body(y_ref, g_ref, b_ref, *o_refs, nbp):
    y = y_ref[...]
    mu = jnp.mean(y, axis=-1, keepdims=True)
    d = y - mu
    var = jnp.mean(d * d, axis=-1, keepdims=True)
    rstd = lax.rsqrt(var + LN_EPS)
    o = d * rstd * g_ref[...] + b_ref[...]
    if nbp is None:
        xb_ref, mu_ref, rstd_ref = o_refs
        xb_ref[...] = o.astype(BF16)
        mu_ref[...] = jnp.broadcast_to(mu, mu_ref.shape)
        rstd_ref[...] = jnp.broadcast_to(rstd, rstd_ref.shape)
        return
    oa_ref, ob_ref = o_refs
    i = pl.program_id(0)

    @pl.when(i < nbp)
    def _():
        oa_ref[...] = o

    @pl.when(i >= nbp)
    def _():
        ob_ref[...] = o


def _layer_norm(y, g, b, *, split_rows=None, bm=256):
    m, d = y.shape
    vec = pl.BlockSpec((1, d), lambda i: (0, 0))
    if split_rows is None:
        bm = _pick(m, bm)
        nbp = None
        row = pl.BlockSpec((bm, d), lambda i: (i, 0))
        stat = pl.BlockSpec((bm, LANE), lambda i: (i, 0))
        out_specs = [row, stat, stat]
        out_shape = [jax.ShapeDtypeStruct((m, d), BF16)] + [jax.ShapeDtypeStruct((m, LANE), F32)] * 2
    else:
        mp, ms = split_rows
        bm = math.gcd(math.gcd(mp, ms), bm)
        nbp = mp // bm
        row = pl.BlockSpec((bm, d), lambda i: (i, 0))
        out_specs = [pl.BlockSpec((bm, d), lambda i: (jnp.minimum(i, nbp - 1), 0)),
                     pl.BlockSpec((bm, d), lambda i: (jnp.maximum(i - nbp, 0), 0))]
        out_shape = [jax.ShapeDtypeStruct((mp, d), F32), jax.ShapeDtypeStruct((ms, d), F32)]
    outs = pl.pallas_call(
        functools.partial(_ln_body, nbp=nbp),
        grid=(m // bm,),
        in_specs=[row, vec, vec],
        out_specs=out_specs,
        out_shape=out_shape,
        compiler_params=_params(("arbitrary",)),
        name="layer_norm",
    )(y, g.reshape(1, d), b.reshape(1, d))
    if split_rows is None:
        xb, mu, rstd = outs
        return xb, _LnRes(y, mu, rstd, g, b)
    return outs


def _gla_body(*refs, heads, dk, dv, L, G, has_init, aliased, layer, depth):
    it = iter(refs)
    q_refs, k_refs, v_refs, r_refs, a_refs = ([next(it) for _ in range(G)] for _ in range(5))
    wa_ref, ba_ref, gn_ref = next(it), next(it), next(it)
    s0_ref = next(it) if has_init else None
    if aliased:
        next(it)
    y_ref, sout_ref, s_scr = next(it), next(it), next(it)
    c = pl.program_id(1)

    @pl.when(c == 0)
    def _():
        if has_init:
            s_scr[...] = s0_ref[...]
        else:
            s_scr[...] = jnp.zeros_like(s_scr)

    rows = lax.broadcasted_iota(jnp.int32, (L, L), 0)
    cols = lax.broadcasted_iota(jnp.int32, (L, L), 1)
    causal = rows >= cols
    for g in range(G):
        glog = jnp.dot(a_refs[g][...].astype(BF16), wa_ref[...], preferred_element_type=F32) + ba_ref[...]
        lg = -(jnp.maximum(-glog, 0.0) + jnp.log1p(jnp.exp(-jnp.abs(glog)))) / GLA_TAU
        bcum = jnp.dot(causal.astype(F32), lg, preferred_element_type=F32, precision=lax.Precision.HIGHEST)
        btot_col = lax.dot_general(lg, jnp.ones((L, LANE), F32), (((0,), (0,)), ((), ())),
                                   preferred_element_type=F32, precision=lax.Precision.HIGHEST)
        btot_row = bcum[L - 1:L, :]
        qf = q_refs[g][...] * jnp.exp(bcum) * (dk ** -0.5)
        kf = k_refs[g][...]
        k_in = kf * jnp.exp(-bcum)
        k_dec = kf * jnp.exp(btot_row - bcum)
        v = v_refs[g][...]
        r = r_refs[g][...]
        for h in range(heads):
            ks = slice(h * dk, (h + 1) * dk)
            vs = slice(h * dv, (h + 1) * dv)
            qh = qf[:, ks].astype(BF16)
            vh = v[:, vs].astype(BF16)
            att = lax.dot_general(qh, k_in[:, ks].astype(BF16), (((1,), (1,)), ((), ())),
                                  preferred_element_type=F32)
            att = jnp.where(causal, att, 0.0)
            s_old = s_scr[g, h]
            o = (jnp.dot(qh, s_old.astype(BF16), preferred_element_type=F32)
                 + jnp.dot(att.astype(BF16), vh, preferred_element_type=F32))
            o = o * lax.rsqrt(jnp.mean(o * o, axis=-1, keepdims=True) + RMS_EPS) * gn_ref[:, vs]
            rh = r[:, vs]
            y_ref[g, :, vs] = (o * (rh * (1.0 / (1.0 + jnp.exp(-rh))))).astype(y_ref.dtype)
            decay = jnp.exp(btot_col[ks, :])
            upd = lax.dot_general(k_dec[:, ks].astype(BF16), vh, (((0,), (0,)), ((), ())),
                                  preferred_element_type=F32)
            s_new = jnp.concatenate([s_old[:, j * LANE:(j + 1) * LANE] * decay for j in range(dv // LANE)],
                                    axis=1) + upd
            s_scr[g, h] = s_new
    if aliased:
        sout_ref[...] = s_scr[...]
    else:
        for other in range(depth):
            sout_ref[other] = s_scr[...] if other == layer else jnp.zeros(s_scr.shape, F32)


def _gla(z, a_lr, offs, wa, ba, gn, s0, s_prev, *, depth, layer, nb, T, L, row0, d_gla):
    heads = GLA_HEADS
    dv = d_gla // heads
    dk = dv // 2
    kw = heads * dk
    nc = T // L
    rb0 = row0 // L
    has_init = s0 is not None
    assert row0 % L == 0 and dv % LANE == 0
    assert offs["q_g"] == 0 and offs["k_g"] == kw and offs["v_g"] == d_gla and offs["r_g"] == 2 * d_gla

    G = 2 if nb % 2 == 0 else 1

    def seq_spec(width, col, g):
        return pl.BlockSpec((L, width), lambda b, c: (rb0 + (b * G + g) * nc + c, col))

    in_specs, args = [], []
    for width, col, arr in ((kw, 0, z), (kw, 1, z), (d_gla, 1, z), (d_gla, 2, z), (LANE, 0, a_lr)):
        in_specs += [seq_spec(width, col, g) for g in range(G)]
        args += [arr] * G
    in_specs += [pl.BlockSpec((LANE, kw), lambda b, c: (0, 0)),
                 pl.BlockSpec((1, kw), lambda b, c: (0, 0)),
                 pl.BlockSpec((1, d_gla), lambda b, c: (0, 0))]
    args += [wa, ba, gn]
    state_spec = pl.BlockSpec((None, G, heads, dk, dv), lambda b, c: (layer, b, 0, 0, 0))
    if has_init:
        in_specs.append(state_spec)
        args.append(s0)
    aliases = {}
    aliased = s_prev is not None
    if aliased:
        aliases = {len(args): 1}
        in_specs.append(pl.BlockSpec(memory_space=pl.ANY))
        args.append(s_prev)
        sout_spec = state_spec
    else:
        sout_spec = pl.BlockSpec((depth, G, heads, dk, dv), lambda b, c: (0, b, 0, 0, 0))
    y, s_out = pl.pallas_call(
        functools.partial(_gla_body, heads=heads, dk=dk, dv=dv, L=L, G=G, has_init=has_init,
                          aliased=aliased, layer=layer, depth=depth),
        grid=(nb // G, nc),
        in_specs=in_specs,
        out_specs=[pl.BlockSpec((G, L, d_gla), lambda b, c: (b, c, 0)), sout_spec],
        out_shape=[jax.ShapeDtypeStruct((nb, T, d_gla), BF16),
                   jax.ShapeDtypeStruct((depth, nb, heads, dk, dv), F32)],
        scratch_shapes=[pltpu.VMEM((G, heads, dk, dv), F32)],
        input_output_aliases=aliases,
        compiler_params=_params(("parallel", "arbitrary")),
        name="gla_init" if has_init else "gla",
    )(*args)
    return y.reshape(nb * T, d_gla), s_out


def _pool_body(*refs, Tt, nt, pos0, has_hist, gw):
    it = iter(refs)
    u_ref = next(it)
    h_ref = next(it) if has_hist else None
    w_ref, sc_ref, y_ref, hnew_ref, full = next(it), next(it), next(it), next(it), next(it)
    t = pl.program_id(1)
    H = POOL_HIST + 1

    @pl.when(t == 0)
    def _():
        if has_hist:
            full[0:H, :] = h_ref[...]
        else:
            full[0:H, :] = jnp.zeros((H, full.shape[1]), F32)

    @pl.when(t > 0)
    def _():
        full[0:H, :] = full[Tt:Tt + H, :]

    full[H:H + Tt, :] = u_ref[...]

    @pl.when(t == nt - 1)
    def _():
        hnew_ref[...] = full[Tt:Tt + H, :]

    pos = pos0 + t * Tt + lax.broadcasted_iota(jnp.int32, (Tt, 1), 0)
    for g, w in enumerate(POOL_WINDOWS):
        cs = slice(g * gw, (g + 1) * gw)
        cur = full[H:H + Tt, cs]
        acc = cur
        for j in range(1, w):
            acc = acc + full[H - j:H - j + Tt, cs]
        cnt = jnp.minimum(pos + 1, w).astype(F32)
        pooled = acc / cnt - cur
        y = jnp.dot(pooled.astype(BF16), w_ref[g], preferred_element_type=F32) * sc_ref[:, cs]
        y_ref[:, cs] = y.astype(y_ref.dtype)


def _pool(z, off_u, hist, w_pool, scale, *, nb, T, row0, pos0, d_pool):
    Tt = min(T, 512)
    assert T % Tt == 0 and row0 % Tt == 0 and off_u % d_pool == 0 and Tt >= POOL_HIST + 1
    nt = T // Tt
    rb0 = row0 // Tt
    ng = len(POOL_WINDOWS)
    gw = d_pool // ng
    has_hist = hist is not None
    in_specs = [pl.BlockSpec((Tt, d_pool), lambda b, t: (rb0 + b * nt + t, off_u // d_pool))]
    args = [z]
    if has_hist:
        in_specs.append(pl.BlockSpec((None, POOL_HIST + 1, d_pool), lambda b, t: (b, 0, 0)))
        args.append(hist)
    in_specs += [pl.BlockSpec((ng, gw, gw), lambda b, t: (0, 0, 0)),
                 pl.BlockSpec((1, d_pool), lambda b, t: (0, 0))]
    args += [w_pool, scale]
    return pl.pallas_call(
        functools.partial(_pool_body, Tt=Tt, nt=nt, pos0=pos0, has_hist=has_hist, gw=gw),
        grid=(nb, nt),
        in_specs=in_specs,
        out_specs=[pl.BlockSpec((Tt, d_pool), lambda b, t: (b * nt + t, 0)),
                   pl.BlockSpec((None, POOL_HIST + 1, d_pool), lambda b, t: (b, 0, 0))],
        out_shape=[jax.ShapeDtypeStruct((nb * T, d_pool), BF16),
                   jax.ShapeDtypeStruct((nb, POOL_HIST + 1, d_pool), F32)],
        scratch_shapes=[pltpu.VMEM((Tt + POOL_HIST + 1, d_pool), F32)],
        compiler_params=_params(("parallel", "arbitrary")),
        name="pool_hist" if has_hist else "pool",
    )(*args)


def _lam_init(layer):
    return 0.8 - 0.6 * math.exp(-0.3 * layer)


def _lam(lp_ref, layer):
    lp = lp_ref[...]
    a = jnp.sum(lp[0:1, :] * lp[1:2, :], axis=-1, keepdims=True)
    b = jnp.sum(lp[2:3, :] * lp[3:4, :], axis=-1, keepdims=True)
    return jnp.exp(a) - jnp.exp(b) + _lam_init(layer)


def _rope(x, cos, sin_signed):
    outs = []
    for m in range(2):
        xm = x[:, m * DIFF_DH:(m + 1) * DIFF_DH]
        outs.append(xm * cos + pltpu.roll(xm, DIFF_DH // 2, 1) * sin_signed)
    return jnp.concatenate(outs, axis=1)


LOG2_E = 1.4426950408889634


def _softmax_parts(s, scale):
    m = jnp.max(s, axis=-1, keepdims=True)
    e = jnp.exp2((s - m) * (scale * LOG2_E))
    return e, 1.0 / jnp.sum(e, axis=-1, keepdims=True)


def _diff_finish(o, gn, layer):
    o = o * lax.rsqrt(jnp.mean(o * o, axis=-1, keepdims=True) + RMS_EPS) * gn
    return o * (1.0 - _lam_init(layer))


def _diffp_body(q_ref, k_ref, v_ref, cos_ref, sin_ref, lp_ref, gn_ref, y_ref, qb, kb, vb, *, T, bq, layer):
    dh = DIFF_DH
    lam = _lam(lp_ref, layer)
    cos = cos_ref[...]
    sin = sin_ref[...]
    kb[...] = _rope(k_ref[...], cos, sin).astype(BF16)
    qb[...] = _rope(q_ref[...], cos, sin).astype(BF16)
    vb[...] = v_ref[...].astype(BF16)
    scale = dh ** -0.5
    for i in range(T // bq):
        n = (i + 1) * bq
        qs = slice(i * bq, n)
        qc = (i * bq + lax.broadcasted_iota(jnp.int32, (bq, n), 0)) // CHUNK
        kc = lax.broadcasted_iota(jnp.int32, (bq, n), 1) // CHUNK
        mask = kc <= qc
        es, inv = [], []
        for m in range(2):
            ms = slice(m * dh, (m + 1) * dh)
            s = lax.dot_general(qb[qs, ms], kb[0:n, ms], (((1,), (1,)), ((), ())), preferred_element_type=F32)
            e, r = _softmax_parts(jnp.where(mask, s, NEG_INF), scale)
            es.append(e)
            inv.append(r)
        p = (es[0] * inv[0] - es[1] * (lam * inv[1])).astype(BF16)
        o = jnp.dot(p, vb[0:n, :], preferred_element_type=F32)
        y_ref[qs, :] = _diff_finish(o, gn_ref[...], layer).astype(y_ref.dtype)


def _diff_prompt(z, offs, cos, sin, lp, gn, *, nb, T, layer, d_diff):
    hw = 2 * DIFF_DH
    heads = d_diff // hw
    bq = min(T, 256)
    assert T % bq == 0 and bq % CHUNK == 0
    for name in ("q_d", "k_d", "v_d"):
        assert offs[name] % hw == 0

    def col(name):
        return lambda b, h: (b, offs[name] // hw + h)

    tab = pl.BlockSpec((T, DIFF_DH), lambda b, h: (0, 0))
    return pl.pallas_call(
        functools.partial(_diffp_body, T=T, bq=bq, layer=layer),
        grid=(nb, heads),
        in_specs=[pl.BlockSpec((T, hw), col("q_d")), pl.BlockSpec((T, hw), col("k_d")),
                  pl.BlockSpec((T, hw), col("v_d")), tab, tab,
                  pl.BlockSpec((4, DIFF_DH), lambda b, h: (0, 0)),
                  pl.BlockSpec((1, hw), lambda b, h: (0, h))],
        out_specs=pl.BlockSpec((T, hw), lambda b, h: (b, h)),
        out_shape=jax.ShapeDtypeStruct((nb * T, d_diff), BF16),
        scratch_shapes=[pltpu.VMEM((T, hw), BF16)] * 3,
        compiler_params=_params(("parallel", "parallel")),
        name="diff_prompt",
    )(z, z, z, cos, sin, lp, gn)


def _diffs_body(q_ref, k_ref, v_ref, pk_ref, pv_ref, cos_ref, sin_ref, lp_ref, gn_ref, y_ref,
                qb, s_past, s_new, p_past, acc, *, Ts, kb, nkb, heads, layer):
    dh = DIFF_DH
    hw = 2 * dh
    nmap = 2 * heads
    j = pl.program_id(1)
    scale = dh ** -0.5

    @pl.when(j == 0)
    def _():
        cos = cos_ref[...]
        sin = sin_ref[...]
        for c in range(nmap):
            cs = slice(c * dh, (c + 1) * dh)
            kc = k_ref[:, cs]
            kc = kc * cos + pltpu.roll(kc, dh // 2, 1) * sin
            qc = q_ref[:, cs]
            qc = (qc * cos + pltpu.roll(qc, dh // 2, 1) * sin).astype(BF16)
            qb[:, cs] = qc
            s_new[c] = lax.dot_general(qc, kc.astype(BF16), (((1,), (1,)), ((), ())),
                                       preferred_element_type=F32) * scale

    @pl.when(j < nkb)
    def _():
        for c in range(nmap):
            pk = pk_ref[pl.ds(c, kb, stride=nmap), :].astype(BF16)
            s_past[c, j] = lax.dot_general(qb[:, c * dh:(c + 1) * dh], pk, (((1,), (1,)), ((), ())),
                                           preferred_element_type=F32) * scale

    @pl.when(j == nkb)
    def _():
        lam = _lam(lp_ref, layer)
        for h in range(heads):
            es, en, inv = [], [], []
            for m in range(2):
                sp = s_past[2 * h + m]
                sn = s_new[2 * h + m]
                mx = jnp.maximum(jnp.max(jnp.max(sp, axis=0), axis=-1, keepdims=True),
                                 jnp.max(sn, axis=-1, keepdims=True))
                ep = jnp.exp(sp - mx[None])
                e_n = jnp.exp(sn - mx)
                tot = jnp.sum(jnp.sum(ep, axis=0), axis=-1, keepdims=True) + jnp.sum(e_n, axis=-1, keepdims=True)
                es.append(ep)
                en.append(e_n)
                inv.append(1.0 / tot)
            p_past[h] = (es[0] * inv[0][None] - lam * (es[1] * inv[1][None])).astype(BF16)
            pn = (en[0] * inv[0] - lam * (en[1] * inv[1])).astype(BF16)
            hs = slice(h * hw, (h + 1) * hw)
            acc[:, hs] = jnp.dot(pn, v_ref[:, hs].astype(BF16), preferred_element_type=F32)

    @pl.when(j >= nkb)
    def _():
        for h in range(heads):
            hs = slice(h * hw, (h + 1) * hw)
            acc[:, hs] += jnp.dot(p_past[h, j - nkb], pv_ref[h].astype(BF16), preferred_element_type=F32)

    @pl.when(j == 2 * nkb - 1)
    def _():
        for h in range(heads):
            hs = slice(h * hw, (h + 1) * hw)
            y_ref[:, hs] = _diff_finish(acc[:, hs], gn_ref[:, hs], layer).astype(y_ref.dtype)


def _diff_sample(z, offs, past_k, past_v, cos, sin, lp, gn, *, nb, Ts, row0, layer, d_diff):
    hw = 2 * DIFF_DH
    heads = d_diff // hw
    nmap = 2 * heads
    P = past_v.shape[3]
    kb = _pick(P, 1024)
    nkb = P // kb
    rb0 = row0 // Ts
    assert row0 % Ts == 0 and P % kb == 0
    for name in ("q_d", "k_d", "v_d"):
        assert offs[name] % d_diff == 0

    def col(name):
        return lambda b, j: (rb0 + b, offs[name] // d_diff)

    tab = pl.BlockSpec((Ts, DIFF_DH), lambda b, j: (0, 0))
    return pl.pallas_call(
        functools.partial(_diffs_body, Ts=Ts, kb=kb, nkb=nkb, heads=heads, layer=layer),
        grid=(nb, 2 * nkb),
        in_specs=[pl.BlockSpec((Ts, d_diff), col("q_d")), pl.BlockSpec((Ts, d_diff), col("k_d")),
                  pl.BlockSpec((Ts, d_diff), col("v_d")),
                  pl.BlockSpec((None, None, kb * nmap, DIFF_DH), lambda b, j: (layer, b, jnp.minimum(j, nkb - 1), 0)),
                  pl.BlockSpec((None, None, heads, kb, hw), lambda b, j: (layer, b, 0, jnp.maximum(j - nkb, 0), 0)),
                  tab, tab,
                  pl.BlockSpec((4, DIFF_DH), lambda b, j: (0, 0)),
                  pl.BlockSpec((1, d_diff), lambda b, j: (0, 0))],
        out_specs=pl.BlockSpec((Ts, d_diff), lambda b, j: (b, 0)),
        out_shape=jax.ShapeDtypeStruct((nb * Ts, d_diff), BF16),
        scratch_shapes=[pltpu.VMEM((Ts, d_diff), BF16),
                        pltpu.VMEM((nmap, nkb, Ts, kb), F32),
                        pltpu.VMEM((nmap, Ts, Ts), F32),
                        pltpu.VMEM((heads, nkb, Ts, kb), BF16),
                        pltpu.VMEM((Ts, d_diff), F32)],
        compiler_params=_params(("parallel", "arbitrary")),
        name="diff_sample",
    )(z, z, z, past_k, past_v, cos, sin, lp, gn)


def _kv_out_body(*refs, rt, heads, layer, depth, aliased):
    k_ref, v_ref, cos_ref, sin_ref = refs[:4]
    ko_ref, vo_ref = refs[-2:]
    dh = DIFF_DH
    nmap = 2 * heads
    if not aliased:
        for other in range(depth):
            if other != layer:
                ko_ref[other] = jnp.zeros(ko_ref.shape[1:], F32)
                vo_ref[other] = jnp.zeros(vo_ref.shape[1:], F32)
        ko_ref, vo_ref = ko_ref.at[layer], vo_ref.at[layer]
    cos = cos_ref[...]
    sin = sin_ref[...]
    for c in range(nmap):
        kc = k_ref[:, c * dh:(c + 1) * dh]
        ko_ref[pl.ds(c, rt, stride=nmap), :] = kc * cos + pltpu.roll(kc, dh // 2, 1) * sin
    for h in range(heads):
        vo_ref[h] = v_ref[:, h * 2 * dh:(h + 1) * 2 * dh]


def _kv_out(z, offs, cos, sin, prev, *, depth, layer, nb, T, row0, d_diff):
    hw = 2 * DIFF_DH
    heads = d_diff // hw
    nmap = 2 * heads
    rt = min(T, 512)
    nt = T // rt
    rb0 = row0 // rt
    assert T % rt == 0 and row0 % rt == 0 and offs["k_d"] % d_diff == 0 and offs["v_d"] % d_diff == 0
    aliased = prev is not None
    in_specs = [pl.BlockSpec((rt, d_diff), lambda b, t: (rb0 + b * nt + t, offs["k_d"] // d_diff)),
                pl.BlockSpec((rt, d_diff), lambda b, t: (rb0 + b * nt + t, offs["v_d"] // d_diff)),
                pl.BlockSpec((rt, DIFF_DH), lambda b, t: (t, 0)),
                pl.BlockSpec((rt, DIFF_DH), lambda b, t: (t, 0))]
    args = [z, z, cos, sin]
    aliases = {}
    if aliased:
        in_specs += [pl.BlockSpec(memory_space=pl.ANY)] * 2
        args += list(prev)
        aliases = {4: 0, 5: 1}
    lead, l_idx = (None, layer) if aliased else (depth, 0)
    return pl.pallas_call(
        functools.partial(_kv_out_body, rt=rt, heads=heads, layer=layer, depth=depth, aliased=aliased),
        grid=(nb, nt),
        in_specs=in_specs,
        out_specs=[pl.BlockSpec((lead, rt * nmap, DIFF_DH), lambda b, t: (l_idx, b * nt + t, 0)),
                   pl.BlockSpec((lead, None, heads, rt, hw), lambda b, t: (l_idx, b, 0, t, 0))],
        out_shape=[jax.ShapeDtypeStruct((depth, nb * T * nmap, DIFF_DH), F32),
                   jax.ShapeDtypeStruct((depth, nb, heads, T, hw), F32)],
        input_output_aliases=aliases,
        compiler_params=_params(("parallel", "parallel")),
        name="kv_out",
    )(*args)


def _mem_body(q_ref, k_ref, v_ref, o_ref, *, heads, dh):
    scale = dh ** -0.5
    for h in range(heads):
        hs = slice(h * dh, (h + 1) * dh)
        s = lax.dot_general(q_ref[:, hs], k_ref[:, hs].astype(BF16), (((1,), (1,)), ((), ())),
                            preferred_element_type=F32)
        e, r = _softmax_parts(s, scale)
        p = (e * r).astype(BF16)
        o_ref[:, hs] = jnp.dot(p, v_ref[:, hs].astype(BF16), preferred_element_type=F32).astype(o_ref.dtype)


def _mem_rows_body(q_ref, k_ref, v_ref, o_ref, *, heads, dh, n_mem):
    scale = dh ** -0.5
    nj = dh // LANE
    stride = nj * heads
    for h in range(heads):
        s = None
        for j in range(nj):
            kj = k_ref[pl.ds(j * heads + h, n_mem, stride=stride), :].astype(BF16)
            d = lax.dot_general(q_ref[:, h * dh + j * LANE:h * dh + (j + 1) * LANE], kj, (((1,), (1,)), ((), ())),
                                preferred_element_type=F32)
            s = d if s is None else s + d
        e, r = _softmax_parts(s, scale)
        p = (e * r).astype(BF16)
        for j in range(nj):
            vj = v_ref[pl.ds(j * heads + h, n_mem, stride=stride), :].astype(BF16)
            o_ref[:, h * dh + j * LANE:h * dh + (j + 1) * LANE] = jnp.dot(
                p, vj, preferred_element_type=F32).astype(o_ref.dtype)


def _mem_attn(q, mem_k, mem_v, *, layer, nb, T, row0, heads, n_mem):
    d = q.shape[1]
    bt = min(T, 512)
    nt = T // bt
    rb0 = row0 // bt
    assert T % bt == 0 and row0 % bt == 0
    if mem_k.shape[3] == d:
        body = functools.partial(_mem_body, heads=heads, dh=d // heads)
    else:
        body = functools.partial(_mem_rows_body, heads=heads, dh=d // heads, n_mem=n_mem)
    kv = pl.BlockSpec((None, None) + mem_k.shape[2:], lambda b, t: (layer, b, 0, 0))
    return pl.pallas_call(
        body,
        grid=(nb, nt),
        in_specs=[pl.BlockSpec((bt, d), lambda b, t: (rb0 + b * nt + t, 0)), kv, kv],
        out_specs=pl.BlockSpec((bt, d), lambda b, t: (b * nt + t, 0)),
        out_shape=jax.ShapeDtypeStruct((nb * T, d), BF16),
        compiler_params=_params(("parallel", "parallel")),
        name="mem_attn",
    )(q, mem_k, mem_v)


def _rope_tables(pos):
    inv = ROPE_THETA ** (-jnp.arange(0, DIFF_DH, 2, dtype=F32) / DIFF_DH)
    ang = pos.astype(F32)[:, None] * inv[None, :]
    cos, sin = jnp.cos(ang), jnp.sin(ang)
    return jnp.concatenate([cos, cos], axis=1), jnp.concatenate([-sin, sin], axis=1)


def kernel(x_prompt, x_sample, cache_diff_k, cache_diff_v, state_gla, state_pool, cache_mem_k, cache_mem_v,
           mem_prompt, w_in, gla_w_a, gla_b_a, gla_norm, pool_w, pool_scale, diff_lambda, diff_norm, w_out,
           w_mq, w_mk, w_mv, w_mo, w_gate, w_up, w_down, ln_g, ln_b):
    Bp, Tp, D = x_prompt.shape
    Bs, Ts, _ = x_sample.shape
    depth = w_in.shape[0]
    past_len = cache_diff_k.shape[2]
    n_mem = mem_prompt.shape[1]
    mem_heads = cache_mem_k.shape[3]
    d_gla = gla_norm.shape[1]
    d_pool = pool_scale.shape[1]
    d_diff = diff_norm.shape[1]
    kw = gla_w_a.shape[2]
    lr = gla_w_a.shape[1]
    d_ff = w_gate.shape[2]
    alpha = (2 * depth) ** 0.25
    Mp, Ms = Bp * Tp, Bs * Ts
    assert lr <= LANE

    n_head = 2 * kw + 2 * d_gla
    offs = {"q_g": 0, "k_g": kw, "v_g": 2 * kw, "r_g": 2 * kw + d_gla,
            "q_d": n_head, "k_d": n_head + d_diff, "v_d": n_head + 2 * d_diff, "u_p": n_head + 3 * d_diff}
    assert d_ff % (2 * LANE) == 0
    hw = 2 * DIFF_DH
    nh = d_diff // hw

    w_in_b = w_in.astype(BF16)
    w_main = jnp.concatenate([w_in_b[:, :, :n_head], w_in_b[:, :, n_head + lr:]], axis=2)
    w_alr = jnp.pad(w_in_b[:, :, n_head:n_head + lr], ((0, 0), (0, 0), (0, LANE - lr)))
    wa_all = jnp.pad(gla_w_a, ((0, 0), (0, LANE - lr), (0, 0))).astype(BF16)
    pw_all = pool_w.astype(BF16)
    wo_all, wmo_all, wd_all = (w.astype(BF16) for w in (w_out, w_mo, w_down))

    x = (x_prompt.reshape(Mp, D), x_sample.reshape(Ms, D))
    xb = jnp.concatenate(x, axis=0).astype(BF16)
    memb = mem_prompt.reshape(Bp * n_mem, D).astype(BF16)
    cos_p, sin_p = _rope_tables(jnp.arange(Tp))
    cos_s, sin_s = _rope_tables(past_len + jnp.arange(Ts))
    past_k = cache_diff_k.reshape(depth, Bs, past_len * (d_diff // DIFF_DH), DIFF_DH)
    past_v = jnp.transpose(cache_diff_v, (0, 1, 3, 2, 4))
    def mem_rows(c):
        c = c.reshape(depth, Bs, n_mem, mem_heads, D // mem_heads // LANE, LANE)
        return jnp.transpose(c, (0, 1, 2, 4, 3, 5)).reshape(depth, Bs, n_mem * D // LANE, LANE)

    cmem_k = mem_rows(cache_mem_k)
    cmem_v = mem_rows(cache_mem_v)
    hist_s = jnp.pad(state_pool, ((0, 0), (0, 0), (1, 0), (0, 0)))

    outs = {k: [] for k in ("hp", "mk", "mv", "hs")}
    kv_p = kv_s = S_p = S_s = None
    for l in range(depth):
        last = l == depth - 1
        ba = gla_b_a[l].reshape(1, kw)
        gn_g = gla_norm[l].reshape(1, d_gla)
        gn_d = diff_norm[l].reshape(1, d_diff)
        psc = pool_scale[l].reshape(1, d_pool)

        z = _matmul(xb, w_main, layer=l, name="mm_in", out_dtype=F32, bm=1024, bn=1024)
        a_lr = _matmul(xb, w_alr, layer=l, name="mm_alr", out_dtype=F32, bm=1024, bn=LANE)
        kv_p = _kv_out(z, offs, cos_p, sin_p, kv_p, depth=depth, layer=l, nb=Bp, T=Tp, row0=0, d_diff=d_diff)
        kv_s = _kv_out(z, offs, cos_s, sin_s, kv_s, depth=depth, layer=l, nb=Bs, T=Ts, row0=Mp, d_diff=d_diff)
        yg_p, S_p = _gla(z, a_lr, offs, wa_all[l], ba, gn_g, None, S_p, depth=depth, layer=l,
                         nb=Bp, T=Tp, L=CHUNK, row0=0, d_gla=d_gla)
        yg_s, S_s = _gla(z, a_lr, offs, wa_all[l], ba, gn_g, state_gla, S_s, depth=depth, layer=l,
                         nb=Bs, T=Ts, L=Ts, row0=Mp, d_gla=d_gla)
        yp_p, h_p = _pool(z, offs["u_p"], None, pw_all[l], psc, nb=Bp, T=Tp, row0=0, pos0=0, d_pool=d_pool)
        yp_s, h_s = _pool(z, offs["u_p"], hist_s[l], pw_all[l], psc, nb=Bs, T=Ts, row0=Mp, pos0=past_len,
                          d_pool=d_pool)
        yd_p = _diff_prompt(z, offs, cos_p, sin_p, diff_lambda[l], gn_d, nb=Bp, T=Tp, layer=l, d_diff=d_diff)
        yd_s = _diff_sample(z, offs, past_k, past_v, cos_s, sin_s, diff_lambda[l], gn_d,
                            nb=Bs, Ts=Ts, row0=Mp, layer=l, d_diff=d_diff)
        y = _matmul_split([(yg_p, yg_s), (yp_p, yp_s), (yd_p, yd_s)], wo_all, x, layer=l,
                          name="mm_out", alpha=alpha, bm=1024, bn=512)
        xb, x = _layer_norm(y, ln_g[l, 0], ln_b[l, 0])

        mk_p = _matmul(memb, w_mk, layer=l, name="mm_mk", out_dtype=F32, bm=1024, bn=512)
        mv_p = _matmul(memb, w_mv, layer=l, name="mm_mv", out_dtype=F32, bm=1024, bn=512)
        q = _matmul(xb, w_mq, layer=l, name="mm_mq", out_dtype=BF16, bm=1024, bn=512)
        o_p = _mem_attn(q, mk_p.reshape(1, Bp, n_mem, D), mv_p.reshape(1, Bp, n_mem, D),
                        layer=0, nb=Bp, T=Tp, row0=0, heads=mem_heads, n_mem=n_mem)
        o_s = _mem_attn(q, cmem_k, cmem_v, layer=l, nb=Bs, T=Ts, row0=Mp, heads=mem_heads, n_mem=n_mem)
        y = _matmul_split([(o_p, o_s)], wmo_all, x, layer=l, name="mm_mo", alpha=alpha, bm=1024, bn=512)
        xb, x = _layer_norm(y, ln_g[l, 1], ln_b[l, 1])

        hmid = _matmul(xb, w_gate, w2=w_up, layer=l, name="mm_gate_up", out_dtype=BF16, bm=1024, bn=2 * LANE)
        y = _matmul(hmid, wd_all, layer=l, name="mm_down", out_dtype=F32, bm=512, bn=512, res=x, alpha=alpha)
        if last:
            y_prompt, y_sample = _layer_norm(y, ln_g[l, 2], ln_b[l, 2], split_rows=(Mp, Ms))
        else:
            xb, x = _layer_norm(y, ln_g[l, 2], ln_b[l, 2])

        outs["hp"].append(h_p[:, 1:])
        outs["mk"].append(mk_p.reshape(Bp, n_mem, mem_heads, D // mem_heads))
        outs["mv"].append(mv_p.reshape(Bp, n_mem, mem_heads, D // mem_heads))
        outs["hs"].append(h_s[:, 1:])

    st = {k: jnp.stack(v) for k, v in outs.items()}
    new_k_p = kv_p[0].reshape(depth, Bp, Tp, nh, 2, DIFF_DH)
    new_v_p = jnp.transpose(kv_p[1], (0, 1, 3, 2, 4))
    new_k_s = kv_s[0].reshape(depth, Bs, Ts, nh, 2, DIFF_DH)
    new_v_s = jnp.transpose(kv_s[1], (0, 1, 3, 2, 4))
    return (y_prompt.reshape(Bp, Tp, D), y_sample.reshape(Bs, Ts, D),
            new_k_p, new_v_p, S_p, st["hp"], st["mk"], st["mv"],
            new_k_s, new_v_s, S_s, st["hs"])
```

```python
import functools
import math
from typing import NamedTuple

import jax
import jax.numpy as jnp
from jax import lax
from jax.experimental import pallas as pl
from jax.experimental.pallas import tpu as pltpu

F32 = jnp.float32
BF16 = jnp.bfloat16

LANE = 128
V7X_VMEM_BYTES = 64 * 1024 * 1024
VMEM_LIMIT = V7X_VMEM_BYTES - 8 * 1024 * 1024

CHUNK = 64
GLA_HEADS = 4
GLA_TAU = 16.0
POOL_WINDOWS = (2, 4, 8, 16)
POOL_HIST = max(POOL_WINDOWS) - 1
DIFF_DH = 128
ROPE_THETA = 10000.0
NEG_INF = -1e30
LN_EPS = 1e-5
RMS_EPS = 1e-6


def _params(sem):
    return pltpu.CompilerParams(dimension_semantics=sem, vmem_limit_bytes=VMEM_LIMIT)


def _pick(dim, pref):
    if dim <= pref:
        return dim
    b = pref - pref % LANE
    while b >= LANE:
        if dim % b == 0:
            return b
        b -= LANE
    return dim


class _LnRes(NamedTuple):
    y: jax.Array
    mu: jax.Array
    rstd: jax.Array
    g: jax.Array
    b: jax.Array


def _res_inputs(res, bm, bn, ij):
    tile = pl.BlockSpec((bm, bn), lambda *g: ij(*g))
    if not isinstance(res, _LnRes):
        return [tile], [res]
    d = res.g.shape[-1]
    stat = pl.BlockSpec((bm, LANE), lambda *g: (ij(*g)[0], 0))
    vec = pl.BlockSpec((1, bn), lambda *g: (0, ij(*g)[1]))
    return [tile, stat, stat, vec, vec], [res.y, res.mu, res.rstd, res.g.reshape(1, d), res.b.reshape(1, d)]


def _res_value(refs):
    if len(refs) == 1:
        return refs[0][...]
    y, mu, rstd, g, b = refs
    return (y[...] - mu[:, 0:1]) * rstd[:, 0:1] * g[...] + b[...]


def _mm_body(*refs, nk, gated, n_res, alpha, cast_w, own_stack, aliased):
    it = iter(refs)
    x_ref = next(it)
    w_ref = next(it)
    w2_ref = next(it) if gated else None
    res_refs = [next(it) for _ in range(n_res)]
    if aliased:
        next(it)
    o_ref = next(it)
    acc_ref = next(it) if nk > 1 else None
    acc2_ref = next(it) if (nk > 1 and gated) else None
    wb_ref = next(it) if cast_w else None
    wb2_ref = next(it) if (cast_w and gated) else None

    def finish(p, p2):
        if gated:
            p = p * (1.0 / (1.0 + jnp.exp(-p))) * p2
        if res_refs:
            p = alpha * _res_value(res_refs) + p
        if own_stack is None:
            o_ref[...] = p.astype(o_ref.dtype)
        else:
            layer, depth = own_stack
            for other in range(depth):
                o_ref[other] = p.astype(o_ref.dtype) if other == layer else jnp.zeros(p.shape, o_ref.dtype)

    if cast_w:
        @pl.when(pl.program_id(1) == 0)
        def _():
            wb_ref[...] = w_ref[...].astype(BF16)
            if gated:
                wb2_ref[...] = w2_ref[...].astype(BF16)

        w_ref, w2_ref = wb_ref, wb2_ref

    x = x_ref[...]
    p = jnp.dot(x, w_ref[...], preferred_element_type=F32)
    p2 = jnp.dot(x, w2_ref[...], preferred_element_type=F32) if gated else None
    if nk == 1:
        finish(p, p2)
        return
    k = pl.program_id(2)

    @pl.when(k == 0)
    def _():
        acc_ref[...] = p
        if gated:
            acc2_ref[...] = p2

    @pl.when(k > 0)
    def _():
        acc_ref[...] += p
        if gated:
            acc2_ref[...] += p2

    @pl.when(k == nk - 1)
    def _():
        finish(acc_ref[...], acc2_ref[...] if gated else None)


def _matmul(x, w, *, layer, name, out_dtype, bm, bn, bk=None, w2=None, res=None, alpha=1.0, stack=None):
    m, kdim = x.shape
    n = w.shape[2]
    bm = _pick(m, bm)
    bn = _pick(n, bn)
    bk = kdim if bk is None else _pick(kdim, bk)
    assert m % bm == 0 and n % bn == 0 and kdim % bk == 0 and w.shape[1] == kdim
    nk = kdim // bk
    gated = w2 is not None
    cast_w = w.dtype == F32
    assert not (cast_w and nk > 1)
    w_spec = pl.BlockSpec((None, bk, bn), lambda j, i, k: (layer, k, j))
    in_specs = [pl.BlockSpec((bm, bk), lambda j, i, k: (i, k)), w_spec]
    args = [x, w]
    if gated:
        in_specs.append(w_spec)
        args.append(w2)
    n_res = 0
    if res is not None:
        res_specs, res_args = _res_inputs(res, bm, bn, lambda j, i, k: (i, j))
        n_res = len(res_args)
        in_specs += res_specs
        args += res_args
    scratch = []
    if nk > 1:
        scratch += [pltpu.VMEM((bm, bn), F32)] * (2 if gated else 1)
    if cast_w:
        scratch += [pltpu.VMEM((bk, bn), BF16)] * (2 if gated else 1)
    own_stack, aliased, aliases = None, False, {}
    if stack is None:
        out_spec = pl.BlockSpec((bm, bn), lambda j, i, k: (i, j))
        out_shape = jax.ShapeDtypeStruct((m, n), out_dtype)
    else:
        depth, prev = stack
        out_shape = jax.ShapeDtypeStruct((depth, m, n), out_dtype)
        if prev is None:
            own_stack = (layer, depth)
            out_spec = pl.BlockSpec((depth, bm, bn), lambda j, i, k: (0, i, j))
        else:
            aliased, aliases = True, {len(args): 0}
            in_specs.append(pl.BlockSpec(memory_space=pl.ANY))
            args.append(prev)
            out_spec = pl.BlockSpec((None, bm, bn), lambda j, i, k: (layer, i, j))
    return pl.pallas_call(
        functools.partial(_mm_body, nk=nk, gated=gated, n_res=n_res, alpha=alpha, cast_w=cast_w,
                          own_stack=own_stack, aliased=aliased),
        grid=(n // bn, m // bm, nk),
        in_specs=in_specs,
        out_specs=out_spec,
        out_shape=out_shape,
        scratch_shapes=scratch,
        input_output_aliases=aliases,
        compiler_params=_params(("parallel", "arbitrary" if cast_w else "parallel", "arbitrary")),
        name=name,
    )(*args)


def _mm_split_body(*refs, widths, nbp, alpha, res_split):
    npc = len(widths)
    p_refs, s_refs = refs[:npc], refs[npc:2 * npc]
    rest = refs[2 * npc:]
    w_ref = rest[0]
    o_ref = rest[-1]
    res_p, res_s = ([rest[1]], [rest[2]]) if res_split else (rest[1:-1], rest[1:-1])
    i = pl.program_id(1)

    def run(lhs_refs, res_refs):
        acc, k0 = None, 0
        for r, wd in zip(lhs_refs, widths):
            d = jnp.dot(r[...], w_ref[k0:k0 + wd, :], preferred_element_type=F32)
            acc = d if acc is None else acc + d
            k0 += wd
        o_ref[...] = (alpha * _res_value(res_refs) + acc).astype(o_ref.dtype)

    @pl.when(i < nbp)
    def _():
        run(p_refs, res_p)

    @pl.when(i >= nbp)
    def _():
        run(s_refs, res_s)


def _matmul_split(pieces, w, res, *, layer, name, alpha, bm, bn):
    mp, ms = pieces[0][0].shape[0], pieces[0][1].shape[0]
    widths = tuple(p.shape[1] for p, _ in pieces)
    _, kdim, n = w.shape
    res_split = isinstance(res, tuple) and not isinstance(res, _LnRes)
    assert sum(widths) == kdim
    bm = math.gcd(math.gcd(mp, ms), bm)
    bn = _pick(n, bn)
    nbp = mp // bm

    def p_map(j, i):
        return (jnp.minimum(i, nbp - 1), 0)

    def s_map(j, i):
        return (jnp.maximum(i - nbp, 0), 0)

    in_specs = [pl.BlockSpec((bm, wd), p_map) for wd in widths] + [pl.BlockSpec((bm, wd), s_map) for wd in widths]
    in_specs.append(pl.BlockSpec((None, kdim, bn), lambda j, i: (layer, 0, j)))
    if res_split:
        in_specs += [pl.BlockSpec((bm, bn), lambda j, i: (jnp.minimum(i, nbp - 1), j)),
                     pl.BlockSpec((bm, bn), lambda j, i: (jnp.maximum(i - nbp, 0), j))]
        res_args = list(res)
    else:
        res_specs, res_args = _res_inputs(res, bm, bn, lambda j, i: (i, j))
        in_specs += res_specs
    return pl.pallas_call(
        functools.partial(_mm_split_body, widths=widths, nbp=nbp, alpha=alpha, res_split=res_split),
        grid=(n // bn, (mp + ms) // bm),
        in_specs=in_specs,
        out_specs=pl.BlockSpec((bm, bn), lambda j, i: (i, j)),
        out_shape=jax.ShapeDtypeStruct((mp + ms, n), F32),
        compiler_params=_params(("parallel", "arbitrary")),
        name=name,
    )(*[p for p, _ in pieces], *[s for _, s in pieces], w, *res_args)


def _ln_body(y_ref, g_ref, b_ref, *o_refs, nbp):
    y = y_ref[...]
    mu = jnp.mean(y, axis=-1, keepdims=True)
    d = y - mu
    var = jnp.mean(d * d, axis=-1, keepdims=True)
    rstd = lax.rsqrt(var + LN_EPS)
    o = d * rstd * g_ref[...] + b_ref[...]
    if nbp is None:
        xb_ref, mu_ref, rstd_ref = o_refs
        xb_ref[...] = o.astype(BF16)
        mu_ref[...] = jnp.broadcast_to(mu, mu_ref.shape)
        rstd_ref[...] = jnp.broadcast_to(rstd, rstd_ref.shape)
        return
    oa_ref, ob_ref = o_refs
    i = pl.program_id(0)

    @pl.when(i < nbp)
    def _():
        oa_ref[...] = o

    @pl.when(i >= nbp)
    def _():
        ob_ref[...] = o


def _layer_norm(y, g, b, *, split_rows=None, bm=512):
    m, d = y.shape
    vec = pl.BlockSpec((1, d), lambda i: (0, 0))
    if split_rows is None:
        bm = _pick(m, bm)
        nbp = None
        row = pl.BlockSpec((bm, d), lambda i: (i, 0))
        stat = pl.BlockSpec((bm, LANE), lambda i: (i, 0))
        out_specs = [row, stat, stat]
        out_shape = [jax.ShapeDtypeStruct((m, d), BF16)] + [jax.ShapeDtypeStruct((m, LANE), F32)] * 2
    else:
        mp, ms = split_rows
        bm = math.gcd(math.gcd(mp, ms), bm // 2)
        nbp = mp // bm
        row = pl.BlockSpec((bm, d), lambda i: (i, 0))
        out_specs = [pl.BlockSpec((bm, d), lambda i: (jnp.minimum(i, nbp - 1), 0)),
                     pl.BlockSpec((bm, d), lambda i: (jnp.maximum(i - nbp, 0), 0))]
        out_shape = [jax.ShapeDtypeStruct((mp, d), F32), jax.ShapeDtypeStruct((ms, d), F32)]
    outs = pl.pallas_call(
        functools.partial(_ln_body, nbp=nbp),
        grid=(m // bm,),
        in_specs=[row, vec, vec],
        out_specs=out_specs,
        out_shape=out_shape,
        compiler_params=_params(("arbitrary",)),
        name="layer_norm",
    )(y, g.reshape(1, d), b.reshape(1, d))
    if split_rows is None:
        xb, mu, rstd = outs
        return xb, _LnRes(y, mu, rstd, g, b)
    return outs


def _gla_body(*refs, heads, dk, dv, L, G, has_init, aliased, layer, depth):
    it = iter(refs)
    q_refs, k_refs, v_refs, r_refs, a_refs = ([next(it) for _ in range(G)] for _ in range(5))
    wa_ref, ba_ref, gn_ref = next(it), next(it), next(it)
    s0_ref = next(it) if has_init else None
    if aliased:
        next(it)
    y_ref, sout_ref, s_scr = next(it), next(it), next(it)
    c = pl.program_id(1)

    @pl.when(c == 0)
    def _():
        if has_init:
            s_scr[...] = s0_ref[...]
        else:
            s_scr[...] = jnp.zeros_like(s_scr)

    rows = lax.broadcasted_iota(jnp.int32, (L, L), 0)
    cols = lax.broadcasted_iota(jnp.int32, (L, L), 1)
    causal = rows >= cols
    for g in range(G):
        glog = jnp.dot(a_refs[g][...].astype(BF16), wa_ref[...], preferred_element_type=F32) + ba_ref[...]
        lg = -(jnp.maximum(-glog, 0.0) + jnp.log1p(jnp.exp(-jnp.abs(glog)))) / GLA_TAU
        bcum = jnp.dot(causal.astype(F32), lg, preferred_element_type=F32, precision=lax.Precision.HIGHEST)
        btot_col = lax.dot_general(lg, jnp.ones((L, LANE), F32), (((0,), (0,)), ((), ())),
                                   preferred_element_type=F32, precision=lax.Precision.HIGHEST)
        btot_row = bcum[L - 1:L, :]
        qf = q_refs[g][...] * jnp.exp(bcum) * (dk ** -0.5)
        kf = k_refs[g][...]
        k_in = kf * jnp.exp(-bcum)
        k_dec = kf * jnp.exp(btot_row - bcum)
        v = v_refs[g][...]
        r = r_refs[g][...]
        for h in range(heads):
            ks = slice(h * dk, (h + 1) * dk)
            vs = slice(h * dv, (h + 1) * dv)
            qh = qf[:, ks].astype(BF16)
            vh = v[:, vs].astype(BF16)
            att = lax.dot_general(qh, k_in[:, ks].astype(BF16), (((1,), (1,)), ((), ())),
                                  preferred_element_type=F32)
            att = jnp.where(causal, att, 0.0)
            s_old = s_scr[g, h]
            o = (jnp.dot(qh, s_old.astype(BF16), preferred_element_type=F32)
                 + jnp.dot(att.astype(BF16), vh, preferred_element_type=F32))
            o = o * lax.rsqrt(jnp.mean(o * o, axis=-1, keepdims=True) + RMS_EPS) * gn_ref[:, vs]
            rh = r[:, vs]
            y_ref[g, :, vs] = (o * (rh * (1.0 / (1.0 + jnp.exp(-rh))))).astype(y_ref.dtype)
            decay = jnp.exp(btot_col[ks, :])
            upd = lax.dot_general(k_dec[:, ks].astype(BF16), vh, (((0,), (0,)), ((), ())),
                                  preferred_element_type=F32)
            s_new = jnp.concatenate([s_old[:, j * LANE:(j + 1) * LANE] * decay for j in range(dv // LANE)],
                                    axis=1) + upd
            s_scr[g, h] = s_new
    if aliased:
        sout_ref[...] = s_scr[...]
    else:
        for other in range(depth):
            sout_ref[other] = s_scr[...] if other == layer else jnp.zeros(s_scr.shape, F32)


def _gla(z, a_lr, offs, wa, ba, gn, s0, s_prev, *, depth, layer, nb, T, L, row0, d_gla):
    heads = GLA_HEADS
    dv = d_gla // heads
    dk = dv // 2
    kw = heads * dk
    nc = T // L
    rb0 = row0 // L
    has_init = s0 is not None
    assert row0 % L == 0 and dv % LANE == 0
    assert offs["q_g"] == 0 and offs["k_g"] == kw and offs["v_g"] == d_gla and offs["r_g"] == 2 * d_gla

    G = 2 if nb % 2 == 0 else 1

    def seq_spec(width, col, g):
        return pl.BlockSpec((L, width), lambda b, c: (rb0 + (b * G + g) * nc + c, col))

    in_specs, args = [], []
    for width, col, arr in ((kw, 0, z), (kw, 1, z), (d_gla, 1, z), (d_gla, 2, z), (LANE, 0, a_lr)):
        in_specs += [seq_spec(width, col, g) for g in range(G)]
        args += [arr] * G
    in_specs += [pl.BlockSpec((LANE, kw), lambda b, c: (0, 0)),
                 pl.BlockSpec((1, kw), lambda b, c: (0, 0)),
                 pl.BlockSpec((1, d_gla), lambda b, c: (0, 0))]
    args += [wa, ba, gn]
    state_spec = pl.BlockSpec((None, G, heads, dk, dv), lambda b, c: (layer, b, 0, 0, 0))
    if has_init:
        in_specs.append(state_spec)
        args.append(s0)
    aliases = {}
    aliased = s_prev is not None
    if aliased:
        aliases = {len(args): 1}
        in_specs.append(pl.BlockSpec(memory_space=pl.ANY))
        args.append(s_prev)
        sout_spec = state_spec
    else:
        sout_spec = pl.BlockSpec((depth, G, heads, dk, dv), lambda b, c: (0, b, 0, 0, 0))
    y, s_out = pl.pallas_call(
        functools.partial(_gla_body, heads=heads, dk=dk, dv=dv, L=L, G=G, has_init=has_init,
                          aliased=aliased, layer=layer, depth=depth),
        grid=(nb // G, nc),
        in_specs=in_specs,
        out_specs=[pl.BlockSpec((G, L, d_gla), lambda b, c: (b, c, 0)), sout_spec],
        out_shape=[jax.ShapeDtypeStruct((nb, T, d_gla), BF16),
                   jax.ShapeDtypeStruct((depth, nb, heads, dk, dv), F32)],
        scratch_shapes=[pltpu.VMEM((G, heads, dk, dv), F32)],
        input_output_aliases=aliases,
        compiler_params=_params(("parallel", "arbitrary")),
        name="gla_init" if has_init else "gla",
    )(*args)
    return y.reshape(nb * T, d_gla), s_out


def _pool_body(*refs, Tt, nt, pos0, has_hist, gw):
    it = iter(refs)
    u_ref = next(it)
    h_ref = next(it) if has_hist else None
    w_ref, sc_ref, y_ref, hnew_ref, full = next(it), next(it), next(it), next(it), next(it)
    t = pl.program_id(1)
    H = POOL_HIST + 1

    @pl.when(t == 0)
    def _():
        if has_hist:
            full[0:H, :] = h_ref[...]
        else:
            full[0:H, :] = jnp.zeros((H, full.shape[1]), F32)

    @pl.when(t > 0)
    def _():
        full[0:H, :] = full[Tt:Tt + H, :]

    full[H:H + Tt, :] = u_ref[...]

    @pl.when(t == nt - 1)
    def _():
        hnew_ref[...] = full[Tt:Tt + H, :]

    pos = pos0 + t * Tt + lax.broadcasted_iota(jnp.int32, (Tt, 1), 0)
    for g, w in enumerate(POOL_WINDOWS):
        cs = slice(g * gw, (g + 1) * gw)
        cur = full[H:H + Tt, cs]
        acc = cur
        for j in range(1, w):
            acc = acc + full[H - j:H - j + Tt, cs]
        cnt = jnp.minimum(pos + 1, w).astype(F32)
        pooled = acc / cnt - cur
        y = jnp.dot(pooled.astype(BF16), w_ref[g], preferred_element_type=F32) * sc_ref[:, cs]
        y_ref[:, cs] = y.astype(y_ref.dtype)


def _pool(z, off_u, hist, w_pool, scale, *, nb, T, row0, pos0, d_pool):
    Tt = min(T, 512)
    assert T % Tt == 0 and row0 % Tt == 0 and off_u % d_pool == 0 and Tt >= POOL_HIST + 1
    nt = T // Tt
    rb0 = row0 // Tt
    ng = len(POOL_WINDOWS)
    gw = d_pool // ng
    has_hist = hist is not None
    in_specs = [pl.BlockSpec((Tt, d_pool), lambda b, t: (rb0 + b * nt + t, off_u // d_pool))]
    args = [z]
    if has_hist:
        in_specs.append(pl.BlockSpec((None, POOL_HIST + 1, d_pool), lambda b, t: (b, 0, 0)))
        args.append(hist)
    in_specs += [pl.BlockSpec((ng, gw, gw), lambda b, t: (0, 0, 0)),
                 pl.BlockSpec((1, d_pool), lambda b, t: (0, 0))]
    args += [w_pool, scale]
    return pl.pallas_call(
        functools.partial(_pool_body, Tt=Tt, nt=nt, pos0=pos0, has_hist=has_hist, gw=gw),
        grid=(nb, nt),
        in_specs=in_specs,
        out_specs=[pl.BlockSpec((Tt, d_pool), lambda b, t: (b * nt + t, 0)),
                   pl.BlockSpec((None, POOL_HIST + 1, d_pool), lambda b, t: (b, 0, 0))],
        out_shape=[jax.ShapeDtypeStruct((nb * T, d_pool), BF16),
                   jax.ShapeDtypeStruct((nb, POOL_HIST + 1, d_pool), F32)],
        scratch_shapes=[pltpu.VMEM((Tt + POOL_HIST + 1, d_pool), F32)],
        compiler_params=_params(("parallel", "arbitrary")),
        name="pool_hist" if has_hist else "pool",
    )(*args)


def _lam_init(layer):
    return 0.8 - 0.6 * math.exp(-0.3 * layer)


def _lam(lp_ref, layer):
    lp = lp_ref[...]
    a = jnp.sum(lp[0:1, :] * lp[1:2, :], axis=-1, keepdims=True)
    b = jnp.sum(lp[2:3, :] * lp[3:4, :], axis=-1, keepdims=True)
    return jnp.exp(a) - jnp.exp(b) + _lam_init(layer)


def _rope(x, cos, sin_signed):
    outs = []
    for m in range(2):
        xm = x[:, m * DIFF_DH:(m + 1) * DIFF_DH]
        outs.append(xm * cos + pltpu.roll(xm, DIFF_DH // 2, 1) * sin_signed)
    return jnp.concatenate(outs, axis=1)


LOG2_E = 1.4426950408889634


def _softmax_parts(s, scale):
    m = jnp.max(s, axis=-1, keepdims=True)
    e = jnp.exp2((s - m) * (scale * LOG2_E))
    return e, 1.0 / jnp.sum(e, axis=-1, keepdims=True)


def _diff_finish(o, gn, layer):
    o = o * lax.rsqrt(jnp.mean(o * o, axis=-1, keepdims=True) + RMS_EPS) * gn
    return o * (1.0 - _lam_init(layer))


def _diffp_body(q_ref, k_ref, v_ref, cos_ref, sin_ref, lp_ref, gn_ref, y_ref, qb, kb, vb, *, T, bq, layer):
    dh = DIFF_DH
    lam = _lam(lp_ref, layer)
    cos = cos_ref[...]
    sin = sin_ref[...]
    kb[...] = _rope(k_ref[...], cos, sin).astype(BF16)
    qb[...] = _rope(q_ref[...], cos, sin).astype(BF16)
    vb[...] = v_ref[...].astype(BF16)
    scale = dh ** -0.5
    for i in range(T // bq):
        n = (i + 1) * bq
        qs = slice(i * bq, n)
        qc = (i * bq + lax.broadcasted_iota(jnp.int32, (bq, n), 0)) // CHUNK
        kc = lax.broadcasted_iota(jnp.int32, (bq, n), 1) // CHUNK
        mask = kc <= qc
        es, inv = [], []
        for m in range(2):
            ms = slice(m * dh, (m + 1) * dh)
            s = lax.dot_general(qb[qs, ms], kb[0:n, ms], (((1,), (1,)), ((), ())), preferred_element_type=F32)
            e, r = _softmax_parts(jnp.where(mask, s, NEG_INF), scale)
            es.append(e)
            inv.append(r)
        p = (es[0] * inv[0] - es[1] * (lam * inv[1])).astype(BF16)
        o = jnp.dot(p, vb[0:n, :], preferred_element_type=F32)
        y_ref[qs, :] = _diff_finish(o, gn_ref[...], layer).astype(y_ref.dtype)


def _diff_prompt(z, offs, cos, sin, lp, gn, *, nb, T, layer, d_diff):
    hw = 2 * DIFF_DH
    heads = d_diff // hw
    bq = min(T, 256)
    assert T % bq == 0 and bq % CHUNK == 0
    for name in ("q_d", "k_d", "v_d"):
        assert offs[name] % hw == 0

    def col(name):
        return lambda b, h: (b, offs[name] // hw + h)

    tab = pl.BlockSpec((T, DIFF_DH), lambda b, h: (0, 0))
    return pl.pallas_call(
        functools.partial(_diffp_body, T=T, bq=bq, layer=layer),
        grid=(nb, heads),
        in_specs=[pl.BlockSpec((T, hw), col("q_d")), pl.BlockSpec((T, hw), col("k_d")),
                  pl.BlockSpec((T, hw), col("v_d")), tab, tab,
                  pl.BlockSpec((4, DIFF_DH), lambda b, h: (0, 0)),
                  pl.BlockSpec((1, hw), lambda b, h: (0, h))],
        out_specs=pl.BlockSpec((T, hw), lambda b, h: (b, h)),
        out_shape=jax.ShapeDtypeStruct((nb * T, d_diff), BF16),
        scratch_shapes=[pltpu.VMEM((T, hw), BF16)] * 3,
        compiler_params=_params(("parallel", "parallel")),
        name="diff_prompt",
    )(z, z, z, cos, sin, lp, gn)


def _diffs_body(q_ref, k_ref, v_ref, pk_ref, pv_ref, cos_ref, sin_ref, lp_ref, gn_ref, y_ref,
                qb, s_past, s_new, p_past, acc, *, Ts, kb, nkb, heads, layer):
    dh = DIFF_DH
    hw = 2 * dh
    nmap = 2 * heads
    j = pl.program_id(1)
    scale = dh ** -0.5

    @pl.when(j == 0)
    def _():
        cos = cos_ref[...]
        sin = sin_ref[...]
        for c in range(nmap):
            cs = slice(c * dh, (c + 1) * dh)
            kc = k_ref[:, cs]
            kc = kc * cos + pltpu.roll(kc, dh // 2, 1) * sin
            qc = q_ref[:, cs]
            qc = (qc * cos + pltpu.roll(qc, dh // 2, 1) * sin).astype(BF16)
            qb[:, cs] = qc
            s_new[c] = lax.dot_general(qc, kc.astype(BF16), (((1,), (1,)), ((), ())),
                                       preferred_element_type=F32) * scale

    @pl.when(j < nkb)
    def _():
        for c in range(nmap):
            pk = pk_ref[pl.ds(c, kb, stride=nmap), :].astype(BF16)
            s_past[c, j] = lax.dot_general(qb[:, c * dh:(c + 1) * dh], pk, (((1,), (1,)), ((), ())),
                                           preferred_element_type=F32) * scale

    @pl.when(j == nkb)
    def _():
        lam = _lam(lp_ref, layer)
        for h in range(heads):
            es, en, inv = [], [], []
            for m in range(2):
                sp = s_past[2 * h + m]
                sn = s_new[2 * h + m]
                mx = jnp.maximum(jnp.max(jnp.max(sp, axis=0), axis=-1, keepdims=True),
                                 jnp.max(sn, axis=-1, keepdims=True))
                ep = jnp.exp(sp - mx[None])
                e_n = jnp.exp(sn - mx)
                tot = jnp.sum(jnp.sum(ep, axis=0), axis=-1, keepdims=True) + jnp.sum(e_n, axis=-1, keepdims=True)
                es.append(ep)
                en.append(e_n)
                inv.append(1.0 / tot)
            p_past[h] = (es[0] * inv[0][None] - lam * (es[1] * inv[1][None])).astype(BF16)
            pn = (en[0] * inv[0] - lam * (en[1] * inv[1])).astype(BF16)
            hs = slice(h * hw, (h + 1) * hw)
            acc[:, hs] = jnp.dot(pn, v_ref[:, hs].astype(BF16), preferred_element_type=F32)

    @pl.when(j >= nkb)
    def _():
        for h in range(heads):
            hs = slice(h * hw, (h + 1) * hw)
            acc[:, hs] += jnp.dot(p_past[h, j - nkb], pv_ref[h].astype(BF16), preferred_element_type=F32)

    @pl.when(j == 2 * nkb - 1)
    def _():
        for h in range(heads):
            hs = slice(h * hw, (h + 1) * hw)
            y_ref[:, hs] = _diff_finish(acc[:, hs], gn_ref[:, hs], layer).astype(y_ref.dtype)


def _diff_sample(z, offs, past_k, past_v, cos, sin, lp, gn, *, nb, Ts, row0, layer, d_diff):
    hw = 2 * DIFF_DH
    heads = d_diff // hw
    nmap = 2 * heads
    P = past_v.shape[3]
    kb = _pick(P, 1024)
    nkb = P // kb
    rb0 = row0 // Ts
    assert row0 % Ts == 0 and P % kb == 0
    for name in ("q_d", "k_d", "v_d"):
        assert offs[name] % d_diff == 0

    def col(name):
        return lambda b, j: (rb0 + b, offs[name] // d_diff)

    tab = pl.BlockSpec((Ts, DIFF_DH), lambda b, j: (0, 0))
    return pl.pallas_call(
        functools.partial(_diffs_body, Ts=Ts, kb=kb, nkb=nkb, heads=heads, layer=layer),
        grid=(nb, 2 * nkb),
        in_specs=[pl.BlockSpec((Ts, d_diff), col("q_d")), pl.BlockSpec((Ts, d_diff), col("k_d")),
                  pl.BlockSpec((Ts, d_diff), col("v_d")),
                  pl.BlockSpec((None, None, kb * nmap, DIFF_DH), lambda b, j: (layer, b, jnp.minimum(j, nkb - 1), 0)),
                  pl.BlockSpec((None, None, heads, kb, hw), lambda b, j: (layer, b, 0, jnp.maximum(j - nkb, 0), 0)),
                  tab, tab,
                  pl.BlockSpec((4, DIFF_DH), lambda b, j: (0, 0)),
                  pl.BlockSpec((1, d_diff), lambda b, j: (0, 0))],
        out_specs=pl.BlockSpec((Ts, d_diff), lambda b, j: (b, 0)),
        out_shape=jax.ShapeDtypeStruct((nb * Ts, d_diff), BF16),
        scratch_shapes=[pltpu.VMEM((Ts, d_diff), BF16),
                        pltpu.VMEM((nmap, nkb, Ts, kb), F32),
                        pltpu.VMEM((nmap, Ts, Ts), F32),
                        pltpu.VMEM((heads, nkb, Ts, kb), BF16),
                        pltpu.VMEM((Ts, d_diff), F32)],
        compiler_params=_params(("parallel", "arbitrary")),
        name="diff_sample",
    )(z, z, z, past_k, past_v, cos, sin, lp, gn)


def _kv_out_body(*refs, rt, heads, layer, depth, aliased):
    k_ref, v_ref, cos_ref, sin_ref = refs[:4]
    ko_ref, vo_ref = refs[-2:]
    dh = DIFF_DH
    nmap = 2 * heads
    if not aliased:
        for other in range(depth):
            if other != layer:
                ko_ref[other] = jnp.zeros(ko_ref.shape[1:], F32)
                vo_ref[other] = jnp.zeros(vo_ref.shape[1:], F32)
        ko_ref, vo_ref = ko_ref.at[layer], vo_ref.at[layer]
    cos = cos_ref[...]
    sin = sin_ref[...]
    for c in range(nmap):
        kc = k_ref[:, c * dh:(c + 1) * dh]
        ko_ref[pl.ds(c, rt, stride=nmap), :] = kc * cos + pltpu.roll(kc, dh // 2, 1) * sin
    for h in range(heads):
        vo_ref[h] = v_ref[:, h * 2 * dh:(h + 1) * 2 * dh]


def _kv_out(z, offs, cos, sin, prev, *, depth, layer, nb, T, row0, d_diff):
    hw = 2 * DIFF_DH
    heads = d_diff // hw
    nmap = 2 * heads
    rt = min(T, 512)
    nt = T // rt
    rb0 = row0 // rt
    assert T % rt == 0 and row0 % rt == 0 and offs["k_d"] % d_diff == 0 and offs["v_d"] % d_diff == 0
    aliased = prev is not None
    in_specs = [pl.BlockSpec((rt, d_diff), lambda b, t: (rb0 + b * nt + t, offs["k_d"] // d_diff)),
                pl.BlockSpec((rt, d_diff), lambda b, t: (rb0 + b * nt + t, offs["v_d"] // d_diff)),
                pl.BlockSpec((rt, DIFF_DH), lambda b, t: (t, 0)),
                pl.BlockSpec((rt, DIFF_DH), lambda b, t: (t, 0))]
    args = [z, z, cos, sin]
    aliases = {}
    if aliased:
        in_specs += [pl.BlockSpec(memory_space=pl.ANY)] * 2
        args += list(prev)
        aliases = {4: 0, 5: 1}
    lead, l_idx = (None, layer) if aliased else (depth, 0)
    return pl.pallas_call(
        functools.partial(_kv_out_body, rt=rt, heads=heads, layer=layer, depth=depth, aliased=aliased),
        grid=(nb, nt),
        in_specs=in_specs,
        out_specs=[pl.BlockSpec((lead, rt * nmap, DIFF_DH), lambda b, t: (l_idx, b * nt + t, 0)),
                   pl.BlockSpec((lead, None, heads, rt, hw), lambda b, t: (l_idx, b, 0, t, 0))],
        out_shape=[jax.ShapeDtypeStruct((depth, nb * T * nmap, DIFF_DH), F32),
                   jax.ShapeDtypeStruct((depth, nb, heads, T, hw), F32)],
        input_output_aliases=aliases,
        compiler_params=_params(("parallel", "parallel")),
        name="kv_out",
    )(*args)


def _mem_body(q_ref, k_ref, v_ref, o_ref, *, heads, dh):
    scale = dh ** -0.5
    for h in range(heads):
        hs = slice(h * dh, (h + 1) * dh)
        s = lax.dot_general(q_ref[:, hs], k_ref[:, hs].astype(BF16), (((1,), (1,)), ((), ())),
                            preferred_element_type=F32)
        e, r = _softmax_parts(s, scale)
        p = (e * r).astype(BF16)
        o_ref[:, hs] = jnp.dot(p, v_ref[:, hs].astype(BF16), preferred_element_type=F32).astype(o_ref.dtype)


def _mem_rows_body(q_ref, k_ref, v_ref, o_ref, *, heads, dh, n_mem):
    scale = dh ** -0.5
    nj = dh // LANE
    stride = nj * heads
    for h in range(heads):
        s = None
        for j in range(nj):
            kj = k_ref[pl.ds(j * heads + h, n_mem, stride=stride), :].astype(BF16)
            d = lax.dot_general(q_ref[:, h * dh + j * LANE:h * dh + (j + 1) * LANE], kj, (((1,), (1,)), ((), ())),
                                preferred_element_type=F32)
            s = d if s is None else s + d
        e, r = _softmax_parts(s, scale)
        p = (e * r).astype(BF16)
        for j in range(nj):
            vj = v_ref[pl.ds(j * heads + h, n_mem, stride=stride), :].astype(BF16)
            o_ref[:, h * dh + j * LANE:h * dh + (j + 1) * LANE] = jnp.dot(
                p, vj, preferred_element_type=F32).astype(o_ref.dtype)


def _mem_attn(q, mem_k, mem_v, *, layer, nb, T, row0, heads, n_mem):
    d = q.shape[1]
    bt = min(T, 512)
    nt = T // bt
    rb0 = row0 // bt
    assert T % bt == 0 and row0 % bt == 0
    if mem_k.shape[3] == d:
        body = functools.partial(_mem_body, heads=heads, dh=d // heads)
    else:
        body = functools.partial(_mem_rows_body, heads=heads, dh=d // heads, n_mem=n_mem)
    kv = pl.BlockSpec((None, None) + mem_k.shape[2:], lambda b, t: (layer, b, 0, 0))
    return pl.pallas_call(
        body,
        grid=(nb, nt),
        in_specs=[pl.BlockSpec((bt, d), lambda b, t: (rb0 + b * nt + t, 0)), kv, kv],
        out_specs=pl.BlockSpec((bt, d), lambda b, t: (b * nt + t, 0)),
        out_shape=jax.ShapeDtypeStruct((nb * T, d), BF16),
        compiler_params=_params(("parallel", "parallel")),
        name="mem_attn",
    )(q, mem_k, mem_v)


def _rope_tables(pos):
    inv = ROPE_THETA ** (-jnp.arange(0, DIFF_DH, 2, dtype=F32) / DIFF_DH)
    ang = pos.astype(F32)[:, None] * inv[None, :]
    cos, sin = jnp.cos(ang), jnp.sin(ang)
    return jnp.concatenate([cos, cos], axis=1), jnp.concatenate([-sin, sin], axis=1)


def kernel(x_prompt, x_sample, cache_diff_k, cache_diff_v, state_gla, state_pool, cache_mem_k, cache_mem_v,
           mem_prompt, w_in, gla_w_a, gla_b_a, gla_norm, pool_w, pool_scale, diff_lambda, diff_norm, w_out,
           w_mq, w_mk, w_mv, w_mo, w_gate, w_up, w_down, ln_g, ln_b):
    Bp, Tp, D = x_prompt.shape
    Bs, Ts, _ = x_sample.shape
    depth = w_in.shape[0]
    past_len = cache_diff_k.shape[2]
    n_mem = mem_prompt.shape[1]
    mem_heads = cache_mem_k.shape[3]
    d_gla = gla_norm.shape[1]
    d_pool = pool_scale.shape[1]
    d_diff = diff_norm.shape[1]
    kw = gla_w_a.shape[2]
    lr = gla_w_a.shape[1]
    d_ff = w_gate.shape[2]
    alpha = (2 * depth) ** 0.25
    Mp, Ms = Bp * Tp, Bs * Ts
    assert lr <= LANE

    n_head = 2 * kw + 2 * d_gla
    offs = {"q_g": 0, "k_g": kw, "v_g": 2 * kw, "r_g": 2 * kw + d_gla,
            "q_d": n_head, "k_d": n_head + d_diff, "v_d": n_head + 2 * d_diff, "u_p": n_head + 3 * d_diff}
    assert d_ff % (2 * LANE) == 0
    hw = 2 * DIFF_DH
    nh = d_diff // hw

    w_in_b = w_in.astype(BF16)
    w_main = jnp.concatenate([w_in_b[:, :, :n_head], w_in_b[:, :, n_head + lr:]], axis=2)
    w_alr = jnp.pad(w_in_b[:, :, n_head:n_head + lr], ((0, 0), (0, 0), (0, LANE - lr)))
    wa_all = jnp.pad(gla_w_a, ((0, 0), (0, LANE - lr), (0, 0))).astype(BF16)
    pw_all = pool_w.astype(BF16)
    wo_all, wmo_all, wd_all = (w.astype(BF16) for w in (w_out, w_mo, w_down))

    x = (x_prompt.reshape(Mp, D), x_sample.reshape(Ms, D))
    xb = jnp.concatenate(x, axis=0).astype(BF16)
    memb = mem_prompt.reshape(Bp * n_mem, D).astype(BF16)
    cos_p, sin_p = _rope_tables(jnp.arange(Tp))
    cos_s, sin_s = _rope_tables(past_len + jnp.arange(Ts))
    past_k = cache_diff_k.reshape(depth, Bs, past_len * (d_diff // DIFF_DH), DIFF_DH)
    past_v = jnp.transpose(cache_diff_v, (0, 1, 3, 2, 4))
    def mem_rows(c):
        c = c.reshape(depth, Bs, n_mem, mem_heads, D // mem_heads // LANE, LANE)
        return jnp.transpose(c, (0, 1, 2, 4, 3, 5)).reshape(depth, Bs, n_mem * D // LANE, LANE)

    cmem_k = mem_rows(cache_mem_k)
    cmem_v = mem_rows(cache_mem_v)
    hist_s = jnp.pad(state_pool, ((0, 0), (0, 0), (1, 0), (0, 0)))

    outs = {k: [] for k in ("hp", "hs")}
    kv_p = kv_s = S_p = S_s = mk_p = mv_p = None
    for l in range(depth):
        last = l == depth - 1
        ba = gla_b_a[l].reshape(1, kw)
        gn_g = gla_norm[l].reshape(1, d_gla)
        gn_d = diff_norm[l].reshape(1, d_diff)
        psc = pool_scale[l].reshape(1, d_pool)

        z = _matmul(xb, w_main, layer=l, name="mm_in", out_dtype=F32, bm=1024, bn=1280)
        a_lr = _matmul(xb, w_alr, layer=l, name="mm_alr", out_dtype=F32, bm=1024, bn=LANE)
        kv_p = _kv_out(z, offs, cos_p, sin_p, kv_p, depth=depth, layer=l, nb=Bp, T=Tp, row0=0, d_diff=d_diff)
        kv_s = _kv_out(z, offs, cos_s, sin_s, kv_s, depth=depth, layer=l, nb=Bs, T=Ts, row0=Mp, d_diff=d_diff)
        yg_p, S_p = _gla(z, a_lr, offs, wa_all[l], ba, gn_g, None, S_p, depth=depth, layer=l,
                         nb=Bp, T=Tp, L=CHUNK, row0=0, d_gla=d_gla)
        yg_s, S_s = _gla(z, a_lr, offs, wa_all[l], ba, gn_g, state_gla, S_s, depth=depth, layer=l,
                         nb=Bs, T=Ts, L=Ts, row0=Mp, d_gla=d_gla)
        yp_p, h_p = _pool(z, offs["u_p"], None, pw_all[l], psc, nb=Bp, T=Tp, row0=0, pos0=0, d_pool=d_pool)
        yp_s, h_s = _pool(z, offs["u_p"], hist_s[l], pw_all[l], psc, nb=Bs, T=Ts, row0=Mp, pos0=past_len,
                          d_pool=d_pool)
        yd_p = _diff_prompt(z, offs, cos_p, sin_p, diff_lambda[l], gn_d, nb=Bp, T=Tp, layer=l, d_diff=d_diff)
        yd_s = _diff_sample(z, offs, past_k, past_v, cos_s, sin_s, diff_lambda[l], gn_d,
                            nb=Bs, Ts=Ts, row0=Mp, layer=l, d_diff=d_diff)
        y = _matmul_split([(yg_p, yg_s), (yp_p, yp_s), (yd_p, yd_s)], wo_all, x, layer=l,
                          name="mm_out", alpha=alpha, bm=1024, bn=512)
        xb, x = _layer_norm(y, ln_g[l, 0], ln_b[l, 0])

        mk_p = _matmul(memb, w_mk, layer=l, name="mm_mk", out_dtype=F32, bm=1024, bn=512, stack=(depth, mk_p))
        mv_p = _matmul(memb, w_mv, layer=l, name="mm_mv", out_dtype=F32, bm=1024, bn=512, stack=(depth, mv_p))
        q = _matmul(xb, w_mq, layer=l, name="mm_mq", out_dtype=BF16, bm=1024, bn=512)
        o_p = _mem_attn(q, mk_p.reshape(depth, Bp, n_mem, D), mv_p.reshape(depth, Bp, n_mem, D),
                        layer=l, nb=Bp, T=Tp, row0=0, heads=mem_heads, n_mem=n_mem)
        o_s = _mem_attn(q, cmem_k, cmem_v, layer=l, nb=Bs, T=Ts, row0=Mp, heads=mem_heads, n_mem=n_mem)
        y = _matmul_split([(o_p, o_s)], wmo_all, x, layer=l, name="mm_mo", alpha=alpha, bm=1024, bn=512)
        xb, x = _layer_norm(y, ln_g[l, 1], ln_b[l, 1])

        hmid = _matmul(xb, w_gate, w2=w_up, layer=l, name="mm_gate_up", out_dtype=BF16, bm=1024, bn=2 * LANE)
        y = _matmul(hmid, wd_all, layer=l, name="mm_down", out_dtype=F32, bm=512, bn=512, res=x, alpha=alpha)
        if last:
            y_prompt, y_sample = _layer_norm(y, ln_g[l, 2], ln_b[l, 2], split_rows=(Mp, Ms))
        else:
            xb, x = _layer_norm(y, ln_g[l, 2], ln_b[l, 2])

        outs["hp"].append(h_p[:, 1:])
        outs["hs"].append(h_s[:, 1:])

    st = {k: jnp.stack(v) for k, v in outs.items()}
    new_k_p = kv_p[0].reshape(depth, Bp, Tp, nh, 2, DIFF_DH)
    new_v_p = jnp.transpose(kv_p[1], (0, 1, 3, 2, 4))
    new_k_s = kv_s[0].reshape(depth, Bs, Ts, nh, 2, DIFF_DH)
    new_v_s = jnp.transpose(kv_s[1], (0, 1, 3, 2, 4))
    return (y_prompt.reshape(Bp, Tp, D), y_sample.reshape(Bs, Ts, D),
            new_k_p, new_v_p, S_p, st["hp"],
            mk_p.reshape(depth, Bp, n_mem, mem_heads, D // mem_heads),
            mv_p.reshape(depth, Bp, n_mem, mem_heads, D // mem_heads),
            new_k_s, new_v_s, S_s, st["hs"])
```

```python
import functools
import math
from typing import NamedTuple

import jax
import jax.numpy as jnp
from jax import lax
from jax.experimental import pallas as pl
from jax.experimental.pallas import tpu as pltpu

F32 = jnp.float32
BF16 = jnp.bfloat16

LANE = 128
V7X_VMEM_BYTES = 64 * 1024 * 1024
VMEM_LIMIT = V7X_VMEM_BYTES - 8 * 1024 * 1024

CHUNK = 64
GLA_HEADS = 4
GLA_TAU = 16.0
POOL_WINDOWS = (2, 4, 8, 16)
POOL_HIST = max(POOL_WINDOWS) - 1
DIFF_DH = 128
ROPE_THETA = 10000.0
NEG_INF = -1e30
LN_EPS = 1e-5
RMS_EPS = 1e-6


def _params(sem):
    return pltpu.CompilerParams(dimension_semantics=sem, vmem_limit_bytes=VMEM_LIMIT)


def _pick(dim, pref):
    if dim <= pref:
        return dim
    b = pref - pref % LANE
    while b >= LANE:
        if dim % b == 0:
            return b
        b -= LANE
    return dim


class _LnRes(NamedTuple):
    y: jax.Array
    mu: jax.Array
    rstd: jax.Array
    g: jax.Array
    b: jax.Array


def _res_inputs(res, bm, bn, ij):
    tile = pl.BlockSpec((bm, bn), lambda *g: ij(*g))
    if not isinstance(res, _LnRes):
        return [tile], [res]
    d = res.g.shape[-1]
    stat = pl.BlockSpec((bm, LANE), lambda *g: (ij(*g)[0], 0))
    vec = pl.BlockSpec((1, bn), lambda *g: (0, ij(*g)[1]))
    return [tile, stat, stat, vec, vec], [res.y, res.mu, res.rstd, res.g.reshape(1, d), res.b.reshape(1, d)]


def _res_value(refs):
    if len(refs) == 1:
        return refs[0][...]
    y, mu, rstd, g, b = refs
    return (y[...] - mu[:, 0:1]) * rstd[:, 0:1] * g[...] + b[...]


def _mm_body(*refs, nk, gated, n_res, alpha, cast_w, own_stack, aliased):
    it = iter(refs)
    x_ref = next(it)
    w_ref = next(it)
    w2_ref = next(it) if gated else None
    res_refs = [next(it) for _ in range(n_res)]
    if aliased:
        next(it)
    o_ref = next(it)
    acc_ref = next(it) if nk > 1 else None
    acc2_ref = next(it) if (nk > 1 and gated) else None
    wb_ref = next(it) if cast_w else None
    wb2_ref = next(it) if (cast_w and gated) else None

    def finish(p, p2):
        if gated:
            p = p * (1.0 / (1.0 + jnp.exp(-p))) * p2
        if res_refs:
            p = alpha * _res_value(res_refs) + p
        if own_stack is None:
            o_ref[...] = p.astype(o_ref.dtype)
        else:
            layer, depth = own_stack
            for other in range(depth):
                o_ref[other] = p.astype(o_ref.dtype) if other == layer else jnp.zeros(p.shape, o_ref.dtype)

    if cast_w:
        @pl.when(pl.program_id(1) == 0)
        def _():
            wb_ref[...] = w_ref[...].astype(BF16)
            if gated:
                wb2_ref[...] = w2_ref[...].astype(BF16)

        w_ref, w2_ref = wb_ref, wb2_ref

    x = x_ref[...]
    p = jnp.dot(x, w_ref[...], preferred_element_type=F32)
    p2 = jnp.dot(x, w2_ref[...], preferred_element_type=F32) if gated else None
    if nk == 1:
        finish(p, p2)
        return
    k = pl.program_id(2)

    @pl.when(k == 0)
    def _():
        acc_ref[...] = p
        if gated:
            acc2_ref[...] = p2

    @pl.when(k > 0)
    def _():
        acc_ref[...] += p
        if gated:
            acc2_ref[...] += p2

    @pl.when(k == nk - 1)
    def _():
        finish(acc_ref[...], acc2_ref[...] if gated else None)


def _matmul(x, w, *, layer, name, out_dtype, bm, bn, bk=None, w2=None, res=None, alpha=1.0, stack=None):
    m, kdim = x.shape
    n = w.shape[2]
    bm = _pick(m, bm)
    bn = _pick(n, bn)
    bk = kdim if bk is None else _pick(kdim, bk)
    assert m % bm == 0 and n % bn == 0 and kdim % bk == 0 and w.shape[1] == kdim
    nk = kdim // bk
    gated = w2 is not None
    cast_w = w.dtype == F32
    assert not (cast_w and nk > 1)
    w_spec = pl.BlockSpec((None, bk, bn), lambda j, i, k: (layer, k, j))
    in_specs = [pl.BlockSpec((bm, bk), lambda j, i, k: (i, k)), w_spec]
    args = [x, w]
    if gated:
        in_specs.append(w_spec)
        args.append(w2)
    n_res = 0
    if res is not None:
        res_specs, res_args = _res_inputs(res, bm, bn, lambda j, i, k: (i, j))
        n_res = len(res_args)
        in_specs += res_specs
        args += res_args
    scratch = []
    if nk > 1:
        scratch += [pltpu.VMEM((bm, bn), F32)] * (2 if gated else 1)
    if cast_w:
        scratch += [pltpu.VMEM((bk, bn), BF16)] * (2 if gated else 1)
    own_stack, aliased, aliases = None, False, {}
    if stack is None:
        out_spec = pl.BlockSpec((bm, bn), lambda j, i, k: (i, j))
        out_shape = jax.ShapeDtypeStruct((m, n), out_dtype)
    else:
        depth, prev = stack
        out_shape = jax.ShapeDtypeStruct((depth, m, n), out_dtype)
        if prev is None:
            own_stack = (layer, depth)
            out_spec = pl.BlockSpec((depth, bm, bn), lambda j, i, k: (0, i, j))
        else:
            aliased, aliases = True, {len(args): 0}
            in_specs.append(pl.BlockSpec(memory_space=pl.ANY))
            args.append(prev)
            out_spec = pl.BlockSpec((None, bm, bn), lambda j, i, k: (layer, i, j))
    return pl.pallas_call(
        functools.partial(_mm_body, nk=nk, gated=gated, n_res=n_res, alpha=alpha, cast_w=cast_w,
                          own_stack=own_stack, aliased=aliased),
        grid=(n // bn, m // bm, nk),
        in_specs=in_specs,
        out_specs=out_spec,
        out_shape=out_shape,
        scratch_shapes=scratch,
        input_output_aliases=aliases,
        compiler_params=_params(("parallel", "arbitrary" if cast_w else "parallel", "arbitrary")),
        name=name,
    )(*args)


def _mm_split_body(*refs, widths, nbp, alpha, res_split):
    npc = len(widths)
    p_refs, s_refs = refs[:npc], refs[npc:2 * npc]
    rest = refs[2 * npc:]
    w_ref = rest[0]
    o_ref = rest[-1]
    res_p, res_s = ([rest[1]], [rest[2]]) if res_split else (rest[1:-1], rest[1:-1])
    i = pl.program_id(1)

    def run(lhs_refs, res_refs):
        acc, k0 = None, 0
        for r, wd in zip(lhs_refs, widths):
            d = jnp.dot(r[...], w_ref[k0:k0 + wd, :], preferred_element_type=F32)
            acc = d if acc is None else acc + d
            k0 += wd
        o_ref[...] = (alpha * _res_value(res_refs) + acc).astype(o_ref.dtype)

    @pl.when(i < nbp)
    def _():
        run(p_refs, res_p)

    @pl.when(i >= nbp)
    def _():
        run(s_refs, res_s)


def _matmul_split(pieces, w, res, *, layer, name, alpha, bm, bn):
    mp, ms = pieces[0][0].shape[0], pieces[0][1].shape[0]
    widths = tuple(p.shape[1] for p, _ in pieces)
    _, kdim, n = w.shape
    res_split = isinstance(res, tuple) and not isinstance(res, _LnRes)
    assert sum(widths) == kdim
    bm = math.gcd(math.gcd(mp, ms), bm)
    bn = _pick(n, bn)
    nbp = mp // bm

    def p_map(j, i):
        return (jnp.minimum(i, nbp - 1), 0)

    def s_map(j, i):
        return (jnp.maximum(i - nbp, 0), 0)

    in_specs = [pl.BlockSpec((bm, wd), p_map) for wd in widths] + [pl.BlockSpec((bm, wd), s_map) for wd in widths]
    in_specs.append(pl.BlockSpec((None, kdim, bn), lambda j, i: (layer, 0, j)))
    if res_split:
        in_specs += [pl.BlockSpec((bm, bn), lambda j, i: (jnp.minimum(i, nbp - 1), j)),
                     pl.BlockSpec((bm, bn), lambda j, i: (jnp.maximum(i - nbp, 0), j))]
        res_args = list(res)
    else:
        res_specs, res_args = _res_inputs(res, bm, bn, lambda j, i: (i, j))
        in_specs += res_specs
    return pl.pallas_call(
        functools.partial(_mm_split_body, widths=widths, nbp=nbp, alpha=alpha, res_split=res_split),
        grid=(n // bn, (mp + ms) // bm),
        in_specs=in_specs,
        out_specs=pl.BlockSpec((bm, bn), lambda j, i: (i, j)),
        out_shape=jax.ShapeDtypeStruct((mp + ms, n), F32),
        compiler_params=_params(("parallel", "arbitrary")),
        name=name,
    )(*[p for p, _ in pieces], *[s for _, s in pieces], w, *res_args)


def _ln_body(y_ref, g_ref, b_ref, *o_refs, nbp):
    y = y_ref[...]
    mu = jnp.mean(y, axis=-1, keepdims=True)
    d = y - mu
    var = jnp.mean(d * d, axis=-1, keepdims=True)
    rstd = lax.rsqrt(var + LN_EPS)
    o = d * rstd * g_ref[...] + b_ref[...]
    if nbp is None:
        xb_ref, mu_ref, rstd_ref = o_refs
        xb_ref[...] = o.astype(BF16)
        mu_ref[...] = jnp.broadcast_to(mu, mu_ref.shape)
        rstd_ref[...] = jnp.broadcast_to(rstd, rstd_ref.shape)
        return
    oa_ref, ob_ref = o_refs
    i = pl.program_id(0)

    @pl.when(i < nbp)
    def _():
        oa_ref[...] = o

    @pl.when(i >= nbp)
    def _():
        ob_ref[...] = o


def _layer_norm(y, g, b, *, split_rows=None, bm=512):
    m, d = y.shape
    vec = pl.BlockSpec((1, d), lambda i: (0, 0))
    if split_rows is None:
        bm = _pick(m, bm)
        nbp = None
        row = pl.BlockSpec((bm, d), lambda i: (i, 0))
        stat = pl.BlockSpec((bm, LANE), lambda i: (i, 0))
        out_specs = [row, stat, stat]
        out_shape = [jax.ShapeDtypeStruct((m, d), BF16)] + [jax.ShapeDtypeStruct((m, LANE), F32)] * 2
    else:
        mp, ms = split_rows
        bm = math.gcd(math.gcd(mp, ms), bm // 2)
        nbp = mp // bm
        row = pl.BlockSpec((bm, d), lambda i: (i, 0))
        out_specs = [pl.BlockSpec((bm, d), lambda i: (jnp.minimum(i, nbp - 1), 0)),
                     pl.BlockSpec((bm, d), lambda i: (jnp.maximum(i - nbp, 0), 0))]
        out_shape = [jax.ShapeDtypeStruct((mp, d), F32), jax.ShapeDtypeStruct((ms, d), F32)]
    outs = pl.pallas_call(
        functools.partial(_ln_body, nbp=nbp),
        grid=(m // bm,),
        in_specs=[row, vec, vec],
        out_specs=out_specs,
        out_shape=out_shape,
        compiler_params=_params(("arbitrary",)),
        name="layer_norm",
    )(y, g.reshape(1, d), b.reshape(1, d))
    if split_rows is None:
        xb, mu, rstd = outs
        return xb, _LnRes(y, mu, rstd, g, b)
    return outs


def _gla_body(*refs, heads, dk, dv, L, G, has_init, aliased, layer, depth):
    it = iter(refs)
    q_refs, k_refs, v_refs, r_refs, a_refs = ([next(it) for _ in range(G)] for _ in range(5))
    wa_ref, ba_ref, gn_ref = next(it), next(it), next(it)
    s0_ref = next(it) if has_init else None
    if aliased:
        next(it)
    y_ref, sout_ref, s_scr = next(it), next(it), next(it)
    c = pl.program_id(1)

    @pl.when(c == 0)
    def _():
        if has_init:
            s_scr[...] = s0_ref[...]
        else:
            s_scr[...] = jnp.zeros_like(s_scr)

    rows = lax.broadcasted_iota(jnp.int32, (L, L), 0)
    cols = lax.broadcasted_iota(jnp.int32, (L, L), 1)
    causal = rows >= cols
    for g in range(G):
        glog = jnp.dot(a_refs[g][...].astype(BF16), wa_ref[...], preferred_element_type=F32) + ba_ref[...]
        lg = -(jnp.maximum(-glog, 0.0) + jnp.log1p(jnp.exp(-jnp.abs(glog)))) / GLA_TAU
        bcum = jnp.dot(causal.astype(F32), lg, preferred_element_type=F32, precision=lax.Precision.HIGHEST)
        btot_col = lax.dot_general(lg, jnp.ones((L, LANE), F32), (((0,), (0,)), ((), ())),
                                   preferred_element_type=F32, precision=lax.Precision.HIGHEST)
        btot_row = bcum[L - 1:L, :]
        qf = q_refs[g][...] * jnp.exp(bcum) * (dk ** -0.5)
        kf = k_refs[g][...]
        k_in = kf * jnp.exp(-bcum)
        k_dec = kf * jnp.exp(btot_row - bcum)
        v = v_refs[g][...]
        r = r_refs[g][...]
        for h in range(heads):
            ks = slice(h * dk, (h + 1) * dk)
            vs = slice(h * dv, (h + 1) * dv)
            qh = qf[:, ks].astype(BF16)
            vh = v[:, vs].astype(BF16)
            att = lax.dot_general(qh, k_in[:, ks].astype(BF16), (((1,), (1,)), ((), ())),
                                  preferred_element_type=F32)
            att = jnp.where(causal, att, 0.0)
            s_old = s_scr[g, h]
            o = (jnp.dot(qh, s_old.astype(BF16), preferred_element_type=F32)
                 + jnp.dot(att.astype(BF16), vh, preferred_element_type=F32))
            o = o * lax.rsqrt(jnp.mean(o * o, axis=-1, keepdims=True) + RMS_EPS) * gn_ref[:, vs]
            rh = r[:, vs]
            y_ref[g, :, vs] = (o * (rh * (1.0 / (1.0 + jnp.exp(-rh))))).astype(y_ref.dtype)
            decay = jnp.exp(btot_col[ks, :])
            upd = lax.dot_general(k_dec[:, ks].astype(BF16), vh, (((0,), (0,)), ((), ())),
                                  preferred_element_type=F32)
            s_new = jnp.concatenate([s_old[:, j * LANE:(j + 1) * LANE] * decay for j in range(dv // LANE)],
                                    axis=1) + upd
            s_scr[g, h] = s_new
    if aliased:
        sout_ref[...] = s_scr[...]
    else:
        for other in range(depth):
            sout_ref[other] = s_scr[...] if other == layer else jnp.zeros(s_scr.shape, F32)


def _gla(z, a_lr, offs, wa, ba, gn, s0, s_prev, *, depth, layer, nb, T, L, row0, d_gla):
    heads = GLA_HEADS
    dv = d_gla // heads
    dk = dv // 2
    kw = heads * dk
    nc = T // L
    rb0 = row0 // L
    has_init = s0 is not None
    assert row0 % L == 0 and dv % LANE == 0
    assert offs["q_g"] == 0 and offs["k_g"] == kw and offs["v_g"] == d_gla and offs["r_g"] == 2 * d_gla

    G = 2 if nb % 2 == 0 else 1

    def seq_spec(width, col, g):
        return pl.BlockSpec((L, width), lambda b, c: (rb0 + (b * G + g) * nc + c, col))

    in_specs, args = [], []
    for width, col, arr in ((kw, 0, z), (kw, 1, z), (d_gla, 1, z), (d_gla, 2, z), (LANE, 0, a_lr)):
        in_specs += [seq_spec(width, col, g) for g in range(G)]
        args += [arr] * G
    in_specs += [pl.BlockSpec((LANE, kw), lambda b, c: (0, 0)),
                 pl.BlockSpec((1, kw), lambda b, c: (0, 0)),
                 pl.BlockSpec((1, d_gla), lambda b, c: (0, 0))]
    args += [wa, ba, gn]
    state_spec = pl.BlockSpec((None, G, heads, dk, dv), lambda b, c: (layer, b, 0, 0, 0))
    if has_init:
        in_specs.append(state_spec)
        args.append(s0)
    aliases = {}
    aliased = s_prev is not None
    if aliased:
        aliases = {len(args): 1}
        in_specs.append(pl.BlockSpec(memory_space=pl.ANY))
        args.append(s_prev)
        sout_spec = state_spec
    else:
        sout_spec = pl.BlockSpec((depth, G, heads, dk, dv), lambda b, c: (0, b, 0, 0, 0))
    y, s_out = pl.pallas_call(
        functools.partial(_gla_body, heads=heads, dk=dk, dv=dv, L=L, G=G, has_init=has_init,
                          aliased=aliased, layer=layer, depth=depth),
        grid=(nb // G, nc),
        in_specs=in_specs,
        out_specs=[pl.BlockSpec((G, L, d_gla), lambda b, c: (b, c, 0)), sout_spec],
        out_shape=[jax.ShapeDtypeStruct((nb, T, d_gla), BF16),
                   jax.ShapeDtypeStruct((depth, nb, heads, dk, dv), F32)],
        scratch_shapes=[pltpu.VMEM((G, heads, dk, dv), F32)],
        input_output_aliases=aliases,
        compiler_params=_params(("parallel", "arbitrary")),
        name="gla_init" if has_init else "gla",
    )(*args)
    return y.reshape(nb * T, d_gla), s_out


def _pool_body(*refs, Tt, nt, pos0, has_hist, gw):
    it = iter(refs)
    u_ref = next(it)
    h_ref = next(it) if has_hist else None
    w_ref, sc_ref, y_ref, hnew_ref, full = next(it), next(it), next(it), next(it), next(it)
    t = pl.program_id(1)
    H = POOL_HIST + 1

    @pl.when(t == 0)
    def _():
        if has_hist:
            full[0:H, :] = h_ref[...]
        else:
            full[0:H, :] = jnp.zeros((H, full.shape[1]), F32)

    @pl.when(t > 0)
    def _():
        full[0:H, :] = full[Tt:Tt + H, :]

    full[H:H + Tt, :] = u_ref[...]

    @pl.when(t == nt - 1)
    def _():
        hnew_ref[...] = full[Tt:Tt + H, :]

    pos = pos0 + t * Tt + lax.broadcasted_iota(jnp.int32, (Tt, 1), 0)
    for g, w in enumerate(POOL_WINDOWS):
        cs = slice(g * gw, (g + 1) * gw)
        cur = full[H:H + Tt, cs]
        acc = cur
        for j in range(1, w):
            acc = acc + full[H - j:H - j + Tt, cs]
        cnt = jnp.minimum(pos + 1, w).astype(F32)
        pooled = acc / cnt - cur
        y = jnp.dot(pooled.astype(BF16), w_ref[g], preferred_element_type=F32) * sc_ref[:, cs]
        y_ref[:, cs] = y.astype(y_ref.dtype)


def _pool(z, off_u, hist, w_pool, scale, *, nb, T, row0, pos0, d_pool):
    Tt = min(T, 512)
    assert T % Tt == 0 and row0 % Tt == 0 and off_u % d_pool == 0 and Tt >= POOL_HIST + 1
    nt = T // Tt
    rb0 = row0 // Tt
    ng = len(POOL_WINDOWS)
    gw = d_pool // ng
    has_hist = hist is not None
    in_specs = [pl.BlockSpec((Tt, d_pool), lambda b, t: (rb0 + b * nt + t, off_u // d_pool))]
    args = [z]
    if has_hist:
        in_specs.append(pl.BlockSpec((None, POOL_HIST + 1, d_pool), lambda b, t: (b, 0, 0)))
        args.append(hist)
    in_specs += [pl.BlockSpec((ng, gw, gw), lambda b, t: (0, 0, 0)),
                 pl.BlockSpec((1, d_pool), lambda b, t: (0, 0))]
    args += [w_pool, scale]
    return pl.pallas_call(
        functools.partial(_pool_body, Tt=Tt, nt=nt, pos0=pos0, has_hist=has_hist, gw=gw),
        grid=(nb, nt),
        in_specs=in_specs,
        out_specs=[pl.BlockSpec((Tt, d_pool), lambda b, t: (b * nt + t, 0)),
                   pl.BlockSpec((None, POOL_HIST + 1, d_pool), lambda b, t: (b, 0, 0))],
        out_shape=[jax.ShapeDtypeStruct((nb * T, d_pool), BF16),
                   jax.ShapeDtypeStruct((nb, POOL_HIST + 1, d_pool), F32)],
        scratch_shapes=[pltpu.VMEM((Tt + POOL_HIST + 1, d_pool), F32)],
        compiler_params=_params(("parallel", "arbitrary")),
        name="pool_hist" if has_hist else "pool",
    )(*args)


def _lam_init(layer):
    return 0.8 - 0.6 * math.exp(-0.3 * layer)


def _lam(lp_ref, layer):
    lp = lp_ref[...]
    a = jnp.sum(lp[0:1, :] * lp[1:2, :], axis=-1, keepdims=True)
    b = jnp.sum(lp[2:3, :] * lp[3:4, :], axis=-1, keepdims=True)
    return jnp.exp(a) - jnp.exp(b) + _lam_init(layer)


def _rope(x, cos, sin_signed):
    outs = []
    for m in range(2):
        xm = x[:, m * DIFF_DH:(m + 1) * DIFF_DH]
        outs.append(xm * cos + pltpu.roll(xm, DIFF_DH // 2, 1) * sin_signed)
    return jnp.concatenate(outs, axis=1)


LOG2_E = 1.4426950408889634


def _softmax_parts(s, scale):
    m = jnp.max(s, axis=-1, keepdims=True)
    e = jnp.exp2((s - m) * (scale * LOG2_E))
    return e, 1.0 / jnp.sum(e, axis=-1, keepdims=True)


def _diff_finish(o, gn, layer):
    o = o * lax.rsqrt(jnp.mean(o * o, axis=-1, keepdims=True) + RMS_EPS) * gn
    return o * (1.0 - _lam_init(layer))


def _diffp_body(q_ref, k_ref, v_ref, cos_ref, sin_ref, lp_ref, gn_ref, y_ref, qb, kb, vb, *, T, bq, layer):
    dh = DIFF_DH
    lam = _lam(lp_ref, layer)
    cos = cos_ref[...]
    sin = sin_ref[...]
    kb[...] = _rope(k_ref[...], cos, sin).astype(BF16)
    qb[...] = _rope(q_ref[...], cos, sin).astype(BF16)
    vb[...] = v_ref[...].astype(BF16)
    scale = dh ** -0.5
    for i in range(T // bq):
        n = (i + 1) * bq
        qs = slice(i * bq, n)
        qc = (i * bq + lax.broadcasted_iota(jnp.int32, (bq, n), 0)) // CHUNK
        kc = lax.broadcasted_iota(jnp.int32, (bq, n), 1) // CHUNK
        mask = kc <= qc
        es, inv = [], []
        for m in range(2):
            ms = slice(m * dh, (m + 1) * dh)
            s = lax.dot_general(qb[qs, ms], kb[0:n, ms], (((1,), (1,)), ((), ())), preferred_element_type=F32)
            e, r = _softmax_parts(jnp.where(mask, s, NEG_INF), scale)
            es.append(e)
            inv.append(r)
        p = (es[0] * inv[0] - es[1] * (lam * inv[1])).astype(BF16)
        o = jnp.dot(p, vb[0:n, :], preferred_element_type=F32)
        y_ref[qs, :] = _diff_finish(o, gn_ref[...], layer).astype(y_ref.dtype)


def _diff_prompt(z, offs, cos, sin, lp, gn, *, nb, T, layer, d_diff):
    hw = 2 * DIFF_DH
    heads = d_diff // hw
    bq = min(T, 256)
    assert T % bq == 0 and bq % CHUNK == 0
    for name in ("q_d", "k_d", "v_d"):
        assert offs[name] % hw == 0

    def col(name):
        return lambda b, h: (b, offs[name] // hw + h)

    tab = pl.BlockSpec((T, DIFF_DH), lambda b, h: (0, 0))
    return pl.pallas_call(
        functools.partial(_diffp_body, T=T, bq=bq, layer=layer),
        grid=(nb, heads),
        in_specs=[pl.BlockSpec((T, hw), col("q_d")), pl.BlockSpec((T, hw), col("k_d")),
                  pl.BlockSpec((T, hw), col("v_d")), tab, tab,
                  pl.BlockSpec((4, DIFF_DH), lambda b, h: (0, 0)),
                  pl.BlockSpec((1, hw), lambda b, h: (0, h))],
        out_specs=pl.BlockSpec((T, hw), lambda b, h: (b, h)),
        out_shape=jax.ShapeDtypeStruct((nb * T, d_diff), BF16),
        scratch_shapes=[pltpu.VMEM((T, hw), BF16)] * 3,
        compiler_params=_params(("parallel", "parallel")),
        name="diff_prompt",
    )(z, z, z, cos, sin, lp, gn)


def _diffs_body(q_ref, k_ref, v_ref, pk_ref, pv_ref, cos_ref, sin_ref, lp_ref, gn_ref, y_ref,
                qb, m_scr, l_scr, acc, *, Ts, kb, nkb, heads, layer):
    dh = DIFF_DH
    hw = 2 * dh
    nmap = 2 * heads
    j = pl.program_id(1)
    c2 = dh ** -0.5 * LOG2_E

    @pl.when(j == 0)
    def _():
        cos = cos_ref[...]
        sin = sin_ref[...]
        for c in range(nmap):
            cs = slice(c * dh, (c + 1) * dh)
            hs = slice((c // 2) * hw, (c // 2 + 1) * hw)
            kc = k_ref[:, cs]
            kc = kc * cos + pltpu.roll(kc, dh // 2, 1) * sin
            qc = q_ref[:, cs]
            qc = (qc * cos + pltpu.roll(qc, dh // 2, 1) * sin).astype(BF16)
            qb[:, cs] = qc
            s = lax.dot_general(qc, kc.astype(BF16), (((1,), (1,)), ((), ())), preferred_element_type=F32)
            mx = jnp.max(s, axis=-1, keepdims=True)
            e = jnp.exp2((s - mx) * c2)
            m_scr[c] = mx
            l_scr[c] = jnp.sum(e, axis=-1, keepdims=True)
            acc[c] = jnp.dot(e.astype(BF16), v_ref[:, hs].astype(BF16), preferred_element_type=F32)

    for c in range(nmap):
        pk = pk_ref[pl.ds(c, kb, stride=nmap), :].astype(BF16)
        s = lax.dot_general(qb[:, c * dh:(c + 1) * dh], pk, (((1,), (1,)), ((), ())), preferred_element_type=F32)
        m_old = m_scr[c]
        m_new = jnp.maximum(m_old, jnp.max(s, axis=-1, keepdims=True))
        a = jnp.exp2((m_old - m_new) * c2)
        e = jnp.exp2((s - m_new) * c2)
        m_scr[c] = m_new
        l_scr[c] = l_scr[c] * a + jnp.sum(e, axis=-1, keepdims=True)
        acc[c] = acc[c] * a + jnp.dot(e.astype(BF16), pv_ref[c // 2].astype(BF16), preferred_element_type=F32)

    @pl.when(j == nkb - 1)
    def _():
        lam = _lam(lp_ref, layer)
        for h in range(heads):
            hs = slice(h * hw, (h + 1) * hw)
            o = acc[2 * h] * (1.0 / l_scr[2 * h]) - lam * (acc[2 * h + 1] * (1.0 / l_scr[2 * h + 1]))
            y_ref[:, hs] = _diff_finish(o, gn_ref[:, hs], layer).astype(y_ref.dtype)


def _diff_sample(z, offs, past_k, past_v, cos, sin, lp, gn, *, nb, Ts, row0, layer, d_diff):
    hw = 2 * DIFF_DH
    heads = d_diff // hw
    nmap = 2 * heads
    P = past_v.shape[3]
    kb = _pick(P, 1024)
    nkb = P // kb
    rb0 = row0 // Ts
    assert row0 % Ts == 0 and P % kb == 0
    for name in ("q_d", "k_d", "v_d"):
        assert offs[name] % d_diff == 0

    def col(name):
        return lambda b, j: (rb0 + b, offs[name] // d_diff)

    tab = pl.BlockSpec((Ts, DIFF_DH), lambda b, j: (0, 0))
    return pl.pallas_call(
        functools.partial(_diffs_body, Ts=Ts, kb=kb, nkb=nkb, heads=heads, layer=layer),
        grid=(nb, nkb),
        in_specs=[pl.BlockSpec((Ts, d_diff), col("q_d")), pl.BlockSpec((Ts, d_diff), col("k_d")),
                  pl.BlockSpec((Ts, d_diff), col("v_d")),
                  pl.BlockSpec((None, None, kb * nmap, DIFF_DH), lambda b, j: (layer, b, j, 0)),
                  pl.BlockSpec((None, None, heads, kb, hw), lambda b, j: (layer, b, 0, j, 0)),
                  tab, tab,
                  pl.BlockSpec((4, DIFF_DH), lambda b, j: (0, 0)),
                  pl.BlockSpec((1, d_diff), lambda b, j: (0, 0))],
        out_specs=pl.BlockSpec((Ts, d_diff), lambda b, j: (b, 0)),
        out_shape=jax.ShapeDtypeStruct((nb * Ts, d_diff), BF16),
        scratch_shapes=[pltpu.VMEM((Ts, d_diff), BF16),
                        pltpu.VMEM((nmap, Ts, 1), F32),
                        pltpu.VMEM((nmap, Ts, 1), F32),
                        pltpu.VMEM((nmap, Ts, hw), F32)],
        compiler_params=_params(("parallel", "arbitrary")),
        name="diff_sample",
    )(z, z, z, past_k, past_v, cos, sin, lp, gn)


def _kv_out_body(*refs, rt, heads, layer, depth, aliased):
    k_ref, v_ref, cos_ref, sin_ref = refs[:4]
    ko_ref, vo_ref = refs[-2:]
    dh = DIFF_DH
    nmap = 2 * heads
    if not aliased:
        for other in range(depth):
            if other != layer:
                ko_ref[other] = jnp.zeros(ko_ref.shape[1:], F32)
                vo_ref[other] = jnp.zeros(vo_ref.shape[1:], F32)
        ko_ref, vo_ref = ko_ref.at[layer], vo_ref.at[layer]
    cos = cos_ref[...]
    sin = sin_ref[...]
    for c in range(nmap):
        kc = k_ref[:, c * dh:(c + 1) * dh]
        ko_ref[pl.ds(c, rt, stride=nmap), :] = kc * cos + pltpu.roll(kc, dh // 2, 1) * sin
    for h in range(heads):
        vo_ref[h] = v_ref[:, h * 2 * dh:(h + 1) * 2 * dh]


def _kv_out(z, offs, cos, sin, prev, *, depth, layer, nb, T, row0, d_diff):
    hw = 2 * DIFF_DH
    heads = d_diff // hw
    nmap = 2 * heads
    rt = min(T, 512)
    nt = T // rt
    rb0 = row0 // rt
    assert T % rt == 0 and row0 % rt == 0 and offs["k_d"] % d_diff == 0 and offs["v_d"] % d_diff == 0
    aliased = prev is not None
    in_specs = [pl.BlockSpec((rt, d_diff), lambda b, t: (rb0 + b * nt + t, offs["k_d"] // d_diff)),
                pl.BlockSpec((rt, d_diff), lambda b, t: (rb0 + b * nt + t, offs["v_d"] // d_diff)),
                pl.BlockSpec((rt, DIFF_DH), lambda b, t: (t, 0)),
                pl.BlockSpec((rt, DIFF_DH), lambda b, t: (t, 0))]
    args = [z, z, cos, sin]
    aliases = {}
    if aliased:
        in_specs += [pl.BlockSpec(memory_space=pl.ANY)] * 2
        args += list(prev)
        aliases = {4: 0, 5: 1}
    lead, l_idx = (None, layer) if aliased else (depth, 0)
    return pl.pallas_call(
        functools.partial(_kv_out_body, rt=rt, heads=heads, layer=layer, depth=depth, aliased=aliased),
        grid=(nb, nt),
        in_specs=in_specs,
        out_specs=[pl.BlockSpec((lead, rt * nmap, DIFF_DH), lambda b, t: (l_idx, b * nt + t, 0)),
                   pl.BlockSpec((lead, None, heads, rt, hw), lambda b, t: (l_idx, b, 0, t, 0))],
        out_shape=[jax.ShapeDtypeStruct((depth, nb * T * nmap, DIFF_DH), F32),
                   jax.ShapeDtypeStruct((depth, nb, heads, T, hw), F32)],
        input_output_aliases=aliases,
        compiler_params=_params(("parallel", "parallel")),
        name="kv_out",
    )(*args)


def _mem_body(q_ref, k_ref, v_ref, o_ref, *, heads, dh):
    scale = dh ** -0.5
    for h in range(heads):
        hs = slice(h * dh, (h + 1) * dh)
        s = lax.dot_general(q_ref[:, hs], k_ref[:, hs].astype(BF16), (((1,), (1,)), ((), ())),
                            preferred_element_type=F32)
        e, r = _softmax_parts(s, scale)
        p = (e * r).astype(BF16)
        o_ref[:, hs] = jnp.dot(p, v_ref[:, hs].astype(BF16), preferred_element_type=F32).astype(o_ref.dtype)


def _mem_rows_body(q_ref, k_ref, v_ref, o_ref, *, heads, dh, n_mem):
    scale = dh ** -0.5
    nj = dh // LANE
    stride = nj * heads
    for h in range(heads):
        s = None
        for j in range(nj):
            kj = k_ref[pl.ds(j * heads + h, n_mem, stride=stride), :].astype(BF16)
            d = lax.dot_general(q_ref[:, h * dh + j * LANE:h * dh + (j + 1) * LANE], kj, (((1,), (1,)), ((), ())),
                                preferred_element_type=F32)
            s = d if s is None else s + d
        e, r = _softmax_parts(s, scale)
        p = (e * r).astype(BF16)
        for j in range(nj):
            vj = v_ref[pl.ds(j * heads + h, n_mem, stride=stride), :].astype(BF16)
            o_ref[:, h * dh + j * LANE:h * dh + (j + 1) * LANE] = jnp.dot(
                p, vj, preferred_element_type=F32).astype(o_ref.dtype)


def _mem_attn(q, mem_k, mem_v, *, layer, nb, T, row0, heads, n_mem):
    d = q.shape[1]
    bt = min(T, 512)
    nt = T // bt
    rb0 = row0 // bt
    assert T % bt == 0 and row0 % bt == 0
    if mem_k.shape[3] == d:
        body = functools.partial(_mem_body, heads=heads, dh=d // heads)
    else:
        body = functools.partial(_mem_rows_body, heads=heads, dh=d // heads, n_mem=n_mem)
    kv = pl.BlockSpec((None, None) + mem_k.shape[2:], lambda b, t: (layer, b, 0, 0))
    return pl.pallas_call(
        body,
        grid=(nb, nt),
        in_specs=[pl.BlockSpec((bt, d), lambda b, t: (rb0 + b * nt + t, 0)), kv, kv],
        out_specs=pl.BlockSpec((bt, d), lambda b, t: (b * nt + t, 0)),
        out_shape=jax.ShapeDtypeStruct((nb * T, d), BF16),
        compiler_params=_params(("parallel", "parallel")),
        name="mem_attn",
    )(q, mem_k, mem_v)


def _rope_tables(pos):
    inv = ROPE_THETA ** (-jnp.arange(0, DIFF_DH, 2, dtype=F32) / DIFF_DH)
    ang = pos.astype(F32)[:, None] * inv[None, :]
    cos, sin = jnp.cos(ang), jnp.sin(ang)
    return jnp.concatenate([cos, cos], axis=1), jnp.concatenate([-sin, sin], axis=1)


def kernel(x_prompt, x_sample, cache_diff_k, cache_diff_v, state_gla, state_pool, cache_mem_k, cache_mem_v,
           mem_prompt, w_in, gla_w_a, gla_b_a, gla_norm, pool_w, pool_scale, diff_lambda, diff_norm, w_out,
           w_mq, w_mk, w_mv, w_mo, w_gate, w_up, w_down, ln_g, ln_b):
    Bp, Tp, D = x_prompt.shape
    Bs, Ts, _ = x_sample.shape
    depth = w_in.shape[0]
    past_len = cache_diff_k.shape[2]
    n_mem = mem_prompt.shape[1]
    mem_heads = cache_mem_k.shape[3]
    d_gla = gla_norm.shape[1]
    d_pool = pool_scale.shape[1]
    d_diff = diff_norm.shape[1]
    kw = gla_w_a.shape[2]
    lr = gla_w_a.shape[1]
    d_ff = w_gate.shape[2]
    alpha = (2 * depth) ** 0.25
    Mp, Ms = Bp * Tp, Bs * Ts
    assert lr <= LANE

    n_head = 2 * kw + 2 * d_gla
    offs = {"q_g": 0, "k_g": kw, "v_g": 2 * kw, "r_g": 2 * kw + d_gla,
            "q_d": n_head, "k_d": n_head + d_diff, "v_d": n_head + 2 * d_diff, "u_p": n_head + 3 * d_diff}
    assert d_ff % (2 * LANE) == 0
    hw = 2 * DIFF_DH
    nh = d_diff // hw

    w_in_b = w_in.astype(BF16)
    w_main = jnp.concatenate([w_in_b[:, :, :n_head], w_in_b[:, :, n_head + lr:]], axis=2)
    w_alr = jnp.pad(w_in_b[:, :, n_head:n_head + lr], ((0, 0), (0, 0), (0, LANE - lr)))
    wa_all = jnp.pad(gla_w_a, ((0, 0), (0, LANE - lr), (0, 0))).astype(BF16)
    pw_all = pool_w.astype(BF16)
    wo_all, wmo_all, wd_all = (w.astype(BF16) for w in (w_out, w_mo, w_down))

    x = (x_prompt.reshape(Mp, D), x_sample.reshape(Ms, D))
    xb = jnp.concatenate(x, axis=0).astype(BF16)
    memb = mem_prompt.reshape(Bp * n_mem, D).astype(BF16)
    cos_p, sin_p = _rope_tables(jnp.arange(Tp))
    cos_s, sin_s = _rope_tables(past_len + jnp.arange(Ts))
    past_k = cache_diff_k.reshape(depth, Bs, past_len * (d_diff // DIFF_DH), DIFF_DH)
    past_v = jnp.transpose(cache_diff_v, (0, 1, 3, 2, 4))
    def mem_rows(c):
        c = c.reshape(depth, Bs, n_mem, mem_heads, D // mem_heads // LANE, LANE)
        return jnp.transpose(c, (0, 1, 2, 4, 3, 5)).reshape(depth, Bs, n_mem * D // LANE, LANE)

    cmem_k = mem_rows(cache_mem_k)
    cmem_v = mem_rows(cache_mem_v)
    hist_s = jnp.pad(state_pool, ((0, 0), (0, 0), (1, 0), (0, 0)))

    outs = {k: [] for k in ("hp", "hs")}
    kv_p = kv_s = S_p = S_s = mk_p = mv_p = None
    for l in range(depth):
        last = l == depth - 1
        ba = gla_b_a[l].reshape(1, kw)
        gn_g = gla_norm[l].reshape(1, d_gla)
        gn_d = diff_norm[l].reshape(1, d_diff)
        psc = pool_scale[l].reshape(1, d_pool)

        z = _matmul(xb, w_main, layer=l, name="mm_in", out_dtype=F32, bm=1024, bn=1280)
        a_lr = _matmul(xb, w_alr, layer=l, name="mm_alr", out_dtype=F32, bm=1024, bn=LANE)
        kv_p = _kv_out(z, offs, cos_p, sin_p, kv_p, depth=depth, layer=l, nb=Bp, T=Tp, row0=0, d_diff=d_diff)
        kv_s = _kv_out(z, offs, cos_s, sin_s, kv_s, depth=depth, layer=l, nb=Bs, T=Ts, row0=Mp, d_diff=d_diff)
        yg_p, S_p = _gla(z, a_lr, offs, wa_all[l], ba, gn_g, None, S_p, depth=depth, layer=l,
                         nb=Bp, T=Tp, L=CHUNK, row0=0, d_gla=d_gla)
        yg_s, S_s = _gla(z, a_lr, offs, wa_all[l], ba, gn_g, state_gla, S_s, depth=depth, layer=l,
                         nb=Bs, T=Ts, L=Ts, row0=Mp, d_gla=d_gla)
        yp_p, h_p = _pool(z, offs["u_p"], None, pw_all[l], psc, nb=Bp, T=Tp, row0=0, pos0=0, d_pool=d_pool)
        yp_s, h_s = _pool(z, offs["u_p"], hist_s[l], pw_all[l], psc, nb=Bs, T=Ts, row0=Mp, pos0=past_len,
                          d_pool=d_pool)
        yd_p = _diff_prompt(z, offs, cos_p, sin_p, diff_lambda[l], gn_d, nb=Bp, T=Tp, layer=l, d_diff=d_diff)
        yd_s = _diff_sample(z, offs, past_k, past_v, cos_s, sin_s, diff_lambda[l], gn_d,
                            nb=Bs, Ts=Ts, row0=Mp, layer=l, d_diff=d_diff)
        y = _matmul_split([(yg_p, yg_s), (yp_p, yp_s), (yd_p, yd_s)], wo_all, x, layer=l,
                          name="mm_out", alpha=alpha, bm=1024, bn=512)
        xb, x = _layer_norm(y, ln_g[l, 0], ln_b[l, 0])

        mk_p = _matmul(memb, w_mk, layer=l, name="mm_mk", out_dtype=F32, bm=1024, bn=512, stack=(depth, mk_p))
        mv_p = _matmul(memb, w_mv, layer=l, name="mm_mv", out_dtype=F32, bm=1024, bn=512, stack=(depth, mv_p))
        q = _matmul(xb, w_mq, layer=l, name="mm_mq", out_dtype=BF16, bm=1024, bn=512)
        o_p = _mem_attn(q, mk_p.reshape(depth, Bp, n_mem, D), mv_p.reshape(depth, Bp, n_mem, D),
                        layer=l, nb=Bp, T=Tp, row0=0, heads=mem_heads, n_mem=n_mem)
        o_s = _mem_attn(q, cmem_k, cmem_v, layer=l, nb=Bs, T=Ts, row0=Mp, heads=mem_heads, n_mem=n_mem)
        y = _matmul_split([(o_p, o_s)], wmo_all, x, layer=l, name="mm_mo", alpha=alpha, bm=1024, bn=512)
        xb, x = _layer_norm(y, ln_g[l, 1], ln_b[l, 1])

        hmid = _matmul(xb, w_gate, w2=w_up, layer=l, name="mm_gate_up", out_dtype=BF16, bm=1024, bn=2 * LANE)
        y = _matmul(hmid, wd_all, layer=l, name="mm_down", out_dtype=F32, bm=512, bn=512, res=x, alpha=alpha)
        if last:
            y_prompt, y_sample = _layer_norm(y, ln_g[l, 2], ln_b[l, 2], split_rows=(Mp, Ms))
        else:
            xb, x = _layer_norm(y, ln_g[l, 2], ln_b[l, 2])

        outs["hp"].append(h_p[:, 1:])
        outs["hs"].append(h_s[:, 1:])

    st = {k: jnp.stack(v) for k, v in outs.items()}
    new_k_p = kv_p[0].reshape(depth, Bp, Tp, nh, 2, DIFF_DH)
    new_v_p = jnp.transpose(kv_p[1], (0, 1, 3, 2, 4))
    new_k_s = kv_s[0].reshape(depth, Bs, Ts, nh, 2, DIFF_DH)
    new_v_s = jnp.transpose(kv_s[1], (0, 1, 3, 2, 4))
    return (y_prompt.reshape(Bp, Tp, D), y_sample.reshape(Bs, Ts, D),
            new_k_p, new_v_p, S_p, st["hp"],
            mk_p.reshape(depth, Bp, n_mem, mem_heads, D // mem_heads),
            mv_p.reshape(depth, Bp, n_mem, mem_heads, D // mem_heads),
            new_k_s, new_v_s, S_s, st["hs"])
```
